```python
import math
import jax, jax.numpy as jnp
from jax import lax
import numpy as np

D_MODEL = 1024
BATCH = 8
SEQ = 2048
DEPTH = 4
DEC_BATCH = 128
DEC_SEQ = 4
PAST_LEN = 16384
PAGE_SIZE = 128

N_A_LAYERS = (DEPTH + 1) // 2
N_C_LAYERS = DEPTH // 2

HA_HEADS = 4
HA_DK = 128
HA_DV = 128
HA_FDIM = HA_HEADS * HA_DK
HA_WIDTH = HA_HEADS * HA_DV
SC_WIDTH = D_MODEL - HA_WIDTH
SC_KERNEL = 3
EVEN_IN = 2 * HA_FDIM + 2 * HA_WIDTH + 3 * SC_WIDTH
EVEN_SPLITS = (HA_FDIM, 2 * HA_FDIM, 2 * HA_FDIM + HA_WIDTH, 2 * HA_FDIM + 2 * HA_WIDTH,
               2 * HA_FDIM + 2 * HA_WIDTH + SC_WIDTH, 2 * HA_FDIM + 2 * HA_WIDTH + 2 * SC_WIDTH)
GLA_HEADS = 4
GLA_KEY = D_MODEL // 2
GLA_VAL = D_MODEL
GLA_DK = GLA_KEY // GLA_HEADS
GLA_DV = GLA_VAL // GLA_HEADS
GLA_RANK = 16
GLA_NORMALIZER = 16.0
ODD_IN = 2 * GLA_KEY + 2 * GLA_VAL + GLA_RANK
ODD_SPLITS = (GLA_KEY, 2 * GLA_KEY, 2 * GLA_KEY + GLA_VAL, 2 * GLA_KEY + 2 * GLA_VAL)
MEM_LEN = 256
MEM_HEADS = 4
MEM_HD = D_MODEL // MEM_HEADS
N_GROUPS = 4
EXP_PER_GROUP = 8
TOP_K = 2
D_EXPERT = 512
CHUNK = 64
LN_EPS = 1e-5
RMS_EPS = 1e-6
DN_ALPHA = (2.0 * DEPTH) ** 0.25
DN_BETA = (8.0 * DEPTH) ** -0.25

kernel_name = "hgrn2_conv_gla_memxattn_hmoe_step"


def layer_norm(x, g, b):
    xf = x.astype(jnp.float32)
    mu = jnp.mean(xf, axis=-1, keepdims=True)
    var = jnp.mean(jnp.square(xf - mu), axis=-1, keepdims=True)
    y = (xf - mu) * lax.rsqrt(var + LN_EPS)
    return (y * g.astype(jnp.float32) + b.astype(jnp.float32)).astype(x.dtype)


def rms_norm_swish_gate(o, w, gate):
    of = o.astype(jnp.float32)
    y = of * lax.rsqrt(jnp.mean(of * of, axis=-1, keepdims=True) + RMS_EPS) * w.astype(jnp.float32)
    return (y * jax.nn.silu(gate.astype(jnp.float32))).astype(gate.dtype)


def gated_linear_recurrence(q, k, v, log_a, s0):
    f32 = jnp.float32
    bsz, T, H, _ = q.shape
    dv = v.shape[-1]
    c = min(CHUNK, T)
    n = -(-T // c)
    pad = n * c - T

    def to_chunks(t):
        t = jnp.pad(t.astype(f32), ((0, 0), (0, pad), (0, 0), (0, 0)))
        return t.reshape(bsz, n, c, H, t.shape[-1]).transpose(1, 0, 3, 2, 4)

    causal = jnp.tril(jnp.ones((c, c), dtype=bool))[:, :, None]

    def step(S, blk):
        qb, kb, vb, ab = blk
        b = jnp.cumsum(ab, axis=2)
        diff = b[:, :, :, None, :] - b[:, :, None, :, :]
        decay = jnp.exp(jnp.where(causal, diff, -jnp.inf))
        scores = jnp.einsum('bhtd,bhsd,bhtsd->bhts', qb, kb, decay)
        o = (jnp.einsum('bhts,bhsv->bhtv', scores, vb)
             + jnp.einsum('bhtd,bhdv->bhtv', qb * jnp.exp(b), S))
        b_last = b[:, :, -1:, :]
        S = (jnp.exp(b_last[:, :, 0, :])[..., None] * S
             + jnp.einsum('bhsd,bhsv->bhdv', kb * jnp.exp(b_last - b), vb))
        return S, o

    S, o = lax.scan(step, s0.astype(f32), (to_chunks(q), to_chunks(k), to_chunks(v), to_chunks(log_a)))
    o = o.transpose(1, 0, 3, 2, 4).reshape(bsz, n * c, H, dv)[:, :T]
    return o, S


def hgrn2_mix(q, f, i, g, lb, norm_w, s0):
    bsz, T, _ = q.shape
    z = f.astype(jnp.float32)
    log_f = jnp.logaddexp(jnp.log(lb), jnp.log1p(-lb) + jax.nn.log_sigmoid(z))
    k = (1.0 - lb) * jax.nn.sigmoid(-z)
    qa = jax.nn.silu(q.astype(jnp.float32))
    o, s = gated_linear_recurrence(qa.reshape(bsz, T, HA_HEADS, HA_DK),
                                   k.reshape(bsz, T, HA_HEADS, HA_DK),
                                   i.reshape(bsz, T, HA_HEADS, HA_DV),
                                   log_f.reshape(bsz, T, HA_HEADS, HA_DK), s0)
    o = rms_norm_swish_gate(o, norm_w, g.reshape(bsz, T, HA_HEADS, HA_DV))
    return o.reshape(bsz, T, HA_WIDTH), s


def short_conv_mix(b_gate, c_gate, h, w, buf):
    u = c_gate * h
    ext = jnp.concatenate([buf.astype(u.dtype), u], axis=1)
    T = u.shape[1]
    y = ext[:, 0:T] * w[:, 0]
    for j in range(1, SC_KERNEL):
        y = y + ext[:, j:j + T] * w[:, j]
    return b_gate * y, ext[:, T:]


def gla_mix(q, k, v, g, gz, w_gk2, b_gk2, norm_w, s0):
    bsz, T, _ = q.shape
    log_a = jax.nn.log_sigmoid((gz @ w_gk2 + b_gk2).astype(jnp.float32)) / GLA_NORMALIZER
    qs = q.astype(jnp.float32) * (GLA_DK ** -0.5)
    o, s = gated_linear_recurrence(qs.reshape(bsz, T, GLA_HEADS, GLA_DK),
                                   k.reshape(bsz, T, GLA_HEADS, GLA_DK),
                                   v.reshape(bsz, T, GLA_HEADS, GLA_DV),
                                   log_a.reshape(bsz, T, GLA_HEADS, GLA_DK), s0)
    o = rms_norm_swish_gate(o, norm_w, g.reshape(bsz, T, GLA_HEADS, GLA_DV))
    return o.reshape(bsz, T, GLA_VAL), s


def memory_attention(x, mem_k, mem_v, w_mq, w_mo):
    bsz, T, _ = x.shape
    q = (x @ w_mq).reshape(bsz, T, MEM_HEADS, MEM_HD)
    s = jnp.einsum('bthd,bmhd->bhtm', q, mem_k).astype(jnp.float32) * (MEM_HD ** -0.5)
    p = jax.nn.softmax(s, axis=-1).astype(mem_v.dtype)
    o = jnp.einsum('bhtm,bmhd->bthd', p, mem_v).reshape(bsz, T, MEM_HEADS * MEM_HD)
    return o @ w_mo


def hier_moe(x, w_grp, b_grp, w_er, b_er, w_gate, w_up, w_down):
    bsz, T, D = x.shape
    xt = x.reshape(bsz * T, D)
    grp_logits = (xt @ w_grp + b_grp).astype(jnp.float32)
    grp_prob = jax.nn.softmax(grp_logits, axis=-1)
    g_sel = jnp.argmax(grp_logits, axis=-1)
    g_w = jnp.take_along_axis(grp_prob, g_sel[:, None], axis=-1)
    e_logits = (jnp.einsum('nd,gde->nge', xt, w_er) + b_er).astype(jnp.float32)
    e_logits = jnp.take_along_axis(e_logits, g_sel[:, None, None], axis=1)[:, 0]
    top_v, top_i = lax.top_k(e_logits, TOP_K)
    top_w = jax.nn.softmax(top_v, axis=-1) * g_w
    e_w = jnp.sum(jax.nn.one_hot(top_i, EXP_PER_GROUP, dtype=jnp.float32) * top_w[..., None], axis=1)
    combine = (jax.nn.one_hot(g_sel, N_GROUPS, dtype=jnp.float32)[:, :, None]
               * e_w[:, None, :]).astype(x.dtype)
    out = jnp.zeros_like(xt)
    for gi in range(N_GROUPS):
        hid = (jax.nn.silu(jnp.einsum('nd,edf->nef', xt, w_gate[gi]))
               * jnp.einsum('nd,edf->nef', xt, w_up[gi]))
        out = out + jnp.einsum('nef,efd->nd', hid * combine[:, gi, :, None], w_down[gi])
    return out.reshape(bsz, T, D)


def trunk(x, s_hgrn, s_conv, s_gla, mem_k, mem_v,
          w_in_a, lb_a, conv_w, gnorm_a, w_out_a, w_in_c, w_gk2, b_gk2, gnorm_c, w_out_c,
          w_mq, w_mo, ln_g, ln_b, w_grp, b_grp, w_er, b_er, w_gate, w_up, w_down):
    lb = jnp.cumsum(jax.nn.softmax(lb_a.astype(jnp.float32), axis=0), axis=0)
    lb = lb - lb[0]
    new_hgrn, new_conv, new_gla = [], [], []
    for l in range(DEPTH):
        j = l // 2
        if l % 2 == 0:
            q, f, i, g, bg, cg, h = jnp.split(x @ w_in_a[j], EVEN_SPLITS, axis=-1)
            o_a, s_a = hgrn2_mix(q, f, i, g, lb[j], gnorm_a[j], s_hgrn[j])
            o_b, buf = short_conv_mix(bg, cg, h, conv_w[j], s_conv[j])
            mix = jnp.concatenate([o_a, o_b], axis=-1) @ w_out_a[j]
            new_hgrn.append(s_a)
            new_conv.append(buf)
        else:
            q, k, v, g, gz = jnp.split(x @ w_in_c[j], ODD_SPLITS, axis=-1)
            o_c, s_c = gla_mix(q, k, v, g, gz, w_gk2[j], b_gk2[j], gnorm_c[j], s_gla[j])
            mix = o_c @ w_out_c[j]
            new_gla.append(s_c)
        x = layer_norm(DN_ALPHA * x + mix, ln_g[l, 0], ln_b[l, 0])
        x = layer_norm(DN_ALPHA * x + memory_attention(x, mem_k[l], mem_v[l], w_mq[l], w_mo[l]),
                       ln_g[l, 1], ln_b[l, 1])
        x = layer_norm(DN_ALPHA * x + hier_moe(x, w_grp[l], b_grp[l], w_er[l], b_er[l],
                                               w_gate[l], w_up[l], w_down[l]),
                       ln_g[l, 2], ln_b[l, 2])
    return x, jnp.stack(new_hgrn), jnp.stack(new_conv), jnp.stack(new_gla)


def setup_inputs(seed: int = 0) -> dict:
    key = jax.random.key(seed)
    ks = iter(jax.random.split(key, 40))

    def nrm(shape, scale):
        return jax.random.normal(next(ks), shape, jnp.float32) * scale

    d_in = D_MODEL ** -0.5
    return {
        "x_prompt": nrm((BATCH, SEQ, D_MODEL), 1.0),
        "x_sample": nrm((DEC_BATCH, DEC_SEQ, D_MODEL), 1.0),
        "state_hgrn": nrm((N_A_LAYERS, DEC_BATCH, HA_HEADS, HA_DK, HA_DV), 0.5),
        "state_conv": nrm((N_A_LAYERS, DEC_BATCH, SC_KERNEL - 1, SC_WIDTH), 0.5),
        "state_gla": nrm((N_C_LAYERS, DEC_BATCH, GLA_HEADS, GLA_DK, GLA_DV), 0.3),
        "cache_mem_k": nrm((DEPTH, DEC_BATCH, MEM_LEN, MEM_HEADS, MEM_HD), 1.0),
        "cache_mem_v": nrm((DEPTH, DEC_BATCH, MEM_LEN, MEM_HEADS, MEM_HD), 1.0),
        "mem_prompt": nrm((BATCH, MEM_LEN, D_MODEL), 1.0),
        "w_in_a": nrm((N_A_LAYERS, D_MODEL, EVEN_IN), d_in),
        "lb_a": nrm((N_A_LAYERS, HA_FDIM), 0.5),
        "conv_w": nrm((N_A_LAYERS, SC_WIDTH, SC_KERNEL), SC_KERNEL ** -0.5),
        "gnorm_a": 1.0 + nrm((N_A_LAYERS, HA_DV), 0.02),
        "w_out_a": nrm((N_A_LAYERS, HA_WIDTH + SC_WIDTH, D_MODEL), (HA_WIDTH + SC_WIDTH) ** -0.5 * DN_BETA),
        "w_in_c": nrm((N_C_LAYERS, D_MODEL, ODD_IN), d_in),
        "w_gk2": nrm((N_C_LAYERS, GLA_RANK, GLA_KEY), GLA_RANK ** -0.5),
        "b_gk2": nrm((N_C_LAYERS, GLA_KEY), 0.1),
        "gnorm_c": 1.0 + nrm((N_C_LAYERS, GLA_DV), 0.02),
        "w_out_c": nrm((N_C_LAYERS, GLA_VAL, D_MODEL), GLA_VAL ** -0.5 * DN_BETA),
        "w_mq": nrm((DEPTH, D_MODEL, MEM_HEADS * MEM_HD), d_in),
        "w_mk": nrm((DEPTH, D_MODEL, MEM_HEADS * MEM_HD), d_in),
        "w_mv": nrm((DEPTH, D_MODEL, MEM_HEADS * MEM_HD), d_in),
        "w_mo": nrm((DEPTH, MEM_HEADS * MEM_HD, D_MODEL), (MEM_HEADS * MEM_HD) ** -0.5 * DN_BETA),
        "ln_g": 1.0 + nrm((DEPTH, 3, D_MODEL), 0.02),
        "ln_b": nrm((DEPTH, 3, D_MODEL), 0.02),
        "w_grp": nrm((DEPTH, D_MODEL, N_GROUPS), d_in),
        "b_grp": nrm((DEPTH, N_GROUPS), 0.01),
        "w_er": nrm((DEPTH, N_GROUPS, D_MODEL, EXP_PER_GROUP), d_in),
        "b_er": nrm((DEPTH, N_GROUPS, EXP_PER_GROUP), 0.01),
        "w_gate": nrm((DEPTH, N_GROUPS, EXP_PER_GROUP, D_MODEL, D_EXPERT), d_in),
        "w_up": nrm((DEPTH, N_GROUPS, EXP_PER_GROUP, D_MODEL, D_EXPERT), d_in),
        "w_down": nrm((DEPTH, N_GROUPS, EXP_PER_GROUP, D_EXPERT, D_MODEL), D_EXPERT ** -0.5 * DN_BETA),
    }


def reference(x_prompt, x_sample, state_hgrn, state_conv, state_gla, cache_mem_k, cache_mem_v, mem_prompt,
              w_in_a, lb_a, conv_w, gnorm_a, w_out_a, w_in_c, w_gk2, b_gk2, gnorm_c, w_out_c,
              w_mq, w_mk, w_mv, w_mo, ln_g, ln_b, w_grp, b_grp, w_er, b_er, w_gate, w_up, w_down):
    bsz = x_prompt.shape[0]
    mlen = mem_prompt.shape[1]
    mk_p = jnp.einsum('bmd,ldk->lbmk', mem_prompt, w_mk).reshape(DEPTH, bsz, mlen, MEM_HEADS, MEM_HD)
    mv_p = jnp.einsum('bmd,ldk->lbmk', mem_prompt, w_mv).reshape(DEPTH, bsz, mlen, MEM_HEADS, MEM_HD)
    z_hgrn = jnp.zeros((N_A_LAYERS, bsz, HA_HEADS, HA_DK, HA_DV), jnp.float32)
    z_conv = jnp.zeros((N_A_LAYERS, bsz, SC_KERNEL - 1, SC_WIDTH), x_prompt.dtype)
    z_gla = jnp.zeros((N_C_LAYERS, bsz, GLA_HEADS, GLA_DK, GLA_DV), jnp.float32)
    y_prompt, hgrn_p, conv_p, gla_p = trunk(
        x_prompt, z_hgrn, z_conv, z_gla, mk_p, mv_p,
        w_in_a, lb_a, conv_w, gnorm_a, w_out_a, w_in_c, w_gk2, b_gk2, gnorm_c, w_out_c,
        w_mq, w_mo, ln_g, ln_b, w_grp, b_grp, w_er, b_er, w_gate, w_up, w_down)
    y_sample, hgrn_s, conv_s, gla_s = trunk(
        x_sample, state_hgrn, state_conv, state_gla, cache_mem_k, cache_mem_v,
        w_in_a, lb_a, conv_w, gnorm_a, w_out_a, w_in_c, w_gk2, b_gk2, gnorm_c, w_out_c,
        w_mq, w_mo, ln_g, ln_b, w_grp, b_grp, w_er, b_er, w_gate, w_up, w_down)
    return (y_prompt, y_sample, hgrn_p, conv_p, gla_p, mk_p, mv_p, hgrn_s, conv_s, gla_s)
```

```python
import functools
import math

import numpy as np
import jax
import jax.numpy as jnp
from jax import lax
from jax.experimental import pallas as pl
from jax.experimental.pallas import tpu as pltpu

F32 = jnp.float32
BF16 = jnp.bfloat16

LN_EPS = 1e-5
RMS_EPS = 1e-6
GLA_NORMALIZER = 16.0
TOP_K = 2

LANES = 128
SUBLANES = 8
CHUNK = 128
SAMPLE_ROWS = 8
EXPERT_TILE = 256
VMEM_LIMIT = 48 * 1024 * 1024

_NT = (((1,), (1,)), ((), ()))
_TN = (((0,), (0,)), ((), ()))


def _params(*sem):
    return pltpu.CompilerParams(dimension_semantics=sem, vmem_limit_bytes=VMEM_LIMIT)


def _pick_tile(n, pref):
    t = pref
    while t > SUBLANES and n % t:
        t //= 2
    assert n % t == 0, (n, pref)
    return t


def _dot(a, b):
    return jnp.dot(a, b, preferred_element_type=F32)


def _sigmoid(x):
    return 1.0 / (1.0 + jnp.exp(-x))


def _log_sigmoid(x):
    return jnp.minimum(x, 0.0) - jnp.log1p(jnp.exp(-jnp.abs(x)))


def _split2(a):
    hi = a.astype(BF16)
    lo = (a - hi.astype(F32)).astype(BF16)
    return hi, lo


def _split3(a):
    hi = a.astype(BF16)
    r = a - hi.astype(F32)
    mid = r.astype(BF16)
    lo = (r - mid.astype(F32)).astype(BF16)
    return hi, mid, lo


def _pad_rows(x, n):
    if x.shape[0] == n:
        return x
    return jnp.concatenate([x, jnp.zeros((n - x.shape[0], x.shape[1]), x.dtype)], axis=0)


def _store_token_tiles(ref, y):
    rows, d = y.shape
    nsub = d // LANES
    for s in range(nsub):
        ref[pl.ds(s, rows, stride=nsub), :] = y[:, s * LANES:(s + 1) * LANES]


def _load_token_tiles(ref, rows, d):
    nsub = d // LANES
    return jnp.concatenate([ref[pl.ds(s, rows, stride=nsub), :] for s in range(nsub)], axis=1)


def _mm_kernel(x_ref, w_ref, o_ref):
    o_ref[...] = _dot(x_ref[...], w_ref[...]).astype(o_ref.dtype)


def _matmul(x, w, *, tn, out_dtype=F32, name):
    n, k = x.shape
    m = w.shape[1]
    tm = _pick_tile(n, 512)
    return pl.pallas_call(
        _mm_kernel,
        grid=(n // tm, m // tn),
        in_specs=[pl.BlockSpec((tm, k), lambda i, j: (i, 0)),
                  pl.BlockSpec((k, tn), lambda i, j: (0, j))],
        out_specs=pl.BlockSpec((tm, tn), lambda i, j: (i, j)),
        out_shape=jax.ShapeDtypeStruct((n, m), out_dtype),
        compiler_params=_params("parallel", "parallel"),
        name=name,
    )(x, w)


def _mem_kv_kernel(x_ref, w_ref, o_ref):
    o_ref[0] = _dot(x_ref[...], w_ref[0])


def _mem_kv(mem_bf16, w_all):
    r, d = mem_bf16.shape
    nj = w_all.shape[0]
    tm = _pick_tile(r, 512)
    return pl.pallas_call(
        _mem_kv_kernel,
        grid=(nj, r // tm),
        in_specs=[pl.BlockSpec((tm, d), lambda j, i: (i, 0)),
                  pl.BlockSpec((1, d, w_all.shape[2]), lambda j, i: (j, 0, 0))],
        out_specs=pl.BlockSpec((1, tm, w_all.shape[2]), lambda j, i: (j, i, 0)),
        out_shape=jax.ShapeDtypeStruct((nj, r, w_all.shape[2]), F32),
        compiler_params=_params("parallel", "parallel"),
        name="mem_kv_proj",
    )(mem_bf16, w_all)


def _layer_norm_rows(z, g, b):
    mu = jnp.mean(z, axis=-1, keepdims=True)
    zc = z - mu
    var = jnp.mean(zc * zc, axis=-1, keepdims=True)
    return zc * lax.rsqrt(var + LN_EPS) * g + b


def _route(logits, n_groups, n_exp):
    lane = lax.broadcasted_iota(jnp.int32, logits.shape, 1).astype(F32)
    big = jnp.float32(1 << 20)
    neg = jnp.float32(-jnp.inf)
    gl = jnp.where(lane < n_groups, logits, neg)
    gmax = jnp.max(gl, axis=-1, keepdims=True)
    gsel = jnp.min(jnp.where(gl == gmax, lane, big), axis=-1, keepdims=True)
    gden = jnp.sum(jnp.exp(gl - gmax), axis=-1, keepdims=True)
    g_w = 1.0 / gden
    lo = n_groups + gsel * n_exp
    el = jnp.where((lane >= lo) & (lane < lo + n_exp), logits, neg)
    m1 = jnp.max(el, axis=-1, keepdims=True)
    i1 = jnp.min(jnp.where(el == m1, lane, big), axis=-1, keepdims=True)
    el2 = jnp.where(lane == i1, neg, el)
    m2 = jnp.max(el2, axis=-1, keepdims=True)
    i2 = jnp.min(jnp.where(el2 == m2, lane, big), axis=-1, keepdims=True)
    t = jnp.exp(m2 - m1)
    w1 = g_w / (1.0 + t)
    w2 = g_w * t / (1.0 + t)
    e1 = i1 - n_groups
    e2 = i2 - n_groups
    zero = jnp.zeros(logits.shape, F32)
    return jnp.where(lane == 0, w1, jnp.where(lane == 1, w2, jnp.where(lane == 2, e1, jnp.where(lane == 3, e2, zero))))


def _proj_ln_kernel(*refs, n_in, alpha, router):
    a_refs = refs[:n_in]
    w_refs = refs[n_in:2 * n_in]
    x_ref, g_ref, b_ref = refs[2 * n_in:2 * n_in + 3]
    pos = 2 * n_in + 3
    if router is not None:
        wrh_ref, wrl_ref, br_ref = refs[pos:pos + 3]
        pos += 3
    xf_ref, xb_ref = refs[pos:pos + 2]
    acc = _dot(a_refs[0][...], w_refs[0][...])
    for a_ref, w_ref in zip(a_refs[1:], w_refs[1:]):
        acc = acc + _dot(a_ref[...], w_ref[...])
    y = _layer_norm_rows(alpha * x_ref[...] + acc, g_ref[...], b_ref[...])
    xf_ref[...] = y
    xb_ref[...] = y.astype(BF16)
    if router is not None:
        hi, lo = _split2(y)
        logits = _dot(hi, wrh_ref[...]) + _dot(lo, wrh_ref[...]) + _dot(hi, wrl_ref[...]) + br_ref[...]
        refs[pos + 2][...] = _route(logits, *router)
        _store_token_tiles(refs[pos + 3], y)


def _proj_ln(acts, weights, x, g, b, *, alpha, router_w=None, router=None, name):
    n, d = x.shape
    tm = _pick_tile(n, 512)
    row = lambda i: (i, 0)
    fix = lambda i: (0, 0)
    in_specs = [pl.BlockSpec((tm, a.shape[1]), row) for a in acts]
    in_specs += [pl.BlockSpec(w.shape, fix) for w in weights]
    in_specs += [pl.BlockSpec((tm, d), row), pl.BlockSpec((1, d), fix), pl.BlockSpec((1, d), fix)]
    args = list(acts) + list(weights) + [x, g.reshape(1, d), b.reshape(1, d)]
    out_specs = [pl.BlockSpec((tm, d), row), pl.BlockSpec((tm, d), row)]
    out_shape = [jax.ShapeDtypeStruct((n, d), F32), jax.ShapeDtypeStruct((n, d), BF16)]
    if router_w is not None:
        in_specs += [pl.BlockSpec(r.shape, fix) for r in router_w]
        args += list(router_w)
        out_specs.append(pl.BlockSpec((tm, LANES), row))
        out_shape.append(jax.ShapeDtypeStruct((n, LANES), F32))
        out_specs.append(pl.BlockSpec((tm * (d // LANES), LANES), row))
        out_shape.append(jax.ShapeDtypeStruct((n * (d // LANES), LANES), F32))
    return pl.pallas_call(
        functools.partial(_proj_ln_kernel, n_in=len(acts), alpha=alpha, router=router),
        grid=(n // tm,),
        in_specs=in_specs, out_specs=out_specs, out_shape=out_shape,
        compiler_params=_params("parallel"),
        name=name,
    )(*args)


_LEVELS = (8, 16, 32, 64)


def _rec_constants():
    t = np.arange(CHUNK)
    tri = (t[:, None] >= t[None, :]).astype(np.float32)
    masks = []
    for lvl in _LEVELS:
        bt, bs = t[:, None] // lvl, t[None, :] // lvl
        masks.append(((bt - bs == 1) & (bt % 2 == 1)).astype(np.float32))
    return jnp.asarray(tri, BF16), jnp.asarray(np.stack(masks), F32)


def _rec_chunk(q, k, a, v, st, tri_ref, mask_ref, nv):
    r = SUBLANES * nv
    m = max(r, 2 * SUBLANES)
    dk = q.shape[1]

    hi, mid, lo = _split3(_pad_rows(a, CHUNK))
    tri = tri_ref[0:m, :]
    b = (_dot(tri, hi) + _dot(tri, mid) + _dot(tri, lo))[0:r]
    b_last = b[r - 1:r]

    scores = None
    for li, lvl in enumerate(_LEVELS):
        if lvl >= r:
            continue
        qp, kp = [], []
        for blk in range(r // lvl):
            rows = slice(blk * lvl, (blk + 1) * lvl)
            if blk % 2 == 1:
                rho = b[blk * lvl - 1:blk * lvl]
                qp.append(q[rows] * jnp.exp(b[rows] - rho))
                kp.append(jnp.zeros((lvl, dk), F32))
            else:
                rho = b[(blk + 1) * lvl - 1:(blk + 1) * lvl]
                kp.append(k[rows] * jnp.exp(rho - b[rows]))
                qp.append(jnp.zeros((lvl, dk), F32))
        qh = jnp.concatenate(qp, axis=0).astype(BF16)
        kh = jnp.concatenate(kp, axis=0).astype(BF16)
        s = lax.dot_general(qh, kh, _NT, preferred_element_type=F32) * mask_ref[li]
        scores = s if scores is None else scores + s

    lane = lax.broadcasted_iota(jnp.int32, (SUBLANES, CHUNK), 1)
    sub = lax.broadcasted_iota(jnp.int32, (SUBLANES, CHUNK), 0)
    dparts = []
    for j in range(nv):
        rows = slice(SUBLANES * j, SUBLANES * (j + 1))
        qj, kj, bj = q[rows], k[rows], b[rows]
        acc = jnp.zeros((SUBLANES, CHUNK), F32)
        for s_ in range(SUBLANES):
            e = jnp.exp(jnp.minimum(bj - bj[s_:s_ + 1], 0.0))
            col = jnp.sum(qj * kj[s_:s_ + 1] * e, axis=1, keepdims=True)
            acc = jnp.where(lane == SUBLANES * j + s_, col, acc)
        dparts.append(jnp.where(lane <= sub + SUBLANES * j, acc, 0.0))
    diag = _pad_rows(jnp.concatenate(dparts, axis=0) if nv > 1 else dparts[0], m)
    scores = diag if scores is None else scores + diag

    v_bf = _pad_rows(v, CHUNK).astype(BF16)
    o = _dot(scores.astype(BF16), v_bf)
    qe = _pad_rows(q * jnp.exp(b), m).astype(BF16)
    o = o + lax.dot_general(qe, st.astype(BF16), _NT, preferred_element_type=F32)
    kd = _pad_rows(k * jnp.exp(b_last - b), CHUNK).astype(BF16)
    st_new = st * jnp.exp(b_last) + lax.dot_general(v_bf, kd, _TN, preferred_element_type=F32)
    return o[0:r], st_new


def _rec_kernel(*refs, mode, hb, dk, dv, nv, nchunks, t_valid, qscale):
    q_ref, k_ref, v_ref, g_ref = refs[:4]
    pos = 4
    if mode == "gla":
        gz_ref, wgh_ref, wgl_ref, bg_ref = refs[pos:pos + 4]
        pos += 4
    else:
        loglb_ref, log1m_ref, onem_ref = refs[pos:pos + 3]
        pos += 3
    nw_ref, s0_ref, tri_ref, mask_ref = refs[pos:pos + 4]
    o_ref, so_ref, st_scr = refs[pos + 4:pos + 7]
    r = SUBLANES * nv

    for hh in range(hb):
        st_scr[hh] = s0_ref[0, hh].T

    def chunk_step(c, carry):
        rsl = pl.ds(c * r if isinstance(c, int) else pl.multiple_of(c * r, r), r)
        if t_valid < r:
            valid_k = lax.broadcasted_iota(jnp.int32, (r, dk), 0) < t_valid
            valid_v = lax.broadcasted_iota(jnp.int32, (r, dv), 0) < t_valid
        if mode == "gla":
            gz = _pad_rows(gz_ref[rsl, :], max(r, 2 * SUBLANES))
            gzh, gzl = _split2(gz)
        for hh in range(hb):
            ksl = slice(hh * dk, (hh + 1) * dk)
            vsl = slice(hh * dv, (hh + 1) * dv)
            qr = q_ref[rsl, ksl]
            kr = k_ref[rsl, ksl]
            v = v_ref[rsl, vsl]
            if mode == "gla":
                gk = (_dot(gzh, wgh_ref[:, ksl]) + _dot(gzl, wgh_ref[:, ksl]) + _dot(gzh, wgl_ref[:, ksl]))[0:r]
                a = _log_sigmoid(gk + bg_ref[:, ksl]) / GLA_NORMALIZER
                q = qr * qscale
                k = kr
            else:
                ls = _log_sigmoid(kr)
                x2 = log1m_ref[:, ksl] + ls
                x1 = loglb_ref[:, ksl]
                a = jnp.maximum(x1, x2) + jnp.log1p(jnp.exp(-jnp.abs(x1 - x2)))
                k = onem_ref[:, ksl] * _sigmoid(-kr)
                q = qr * _sigmoid(qr)
            if t_valid < r:
                a = jnp.where(valid_k, a, 0.0)
                k = jnp.where(valid_k, k, 0.0)
                v = jnp.where(valid_v, v, 0.0)
            o, st_new = _rec_chunk(q, k, a, v, st_scr[hh], tri_ref, mask_ref, nv)
            st_scr[hh] = st_new
            g = g_ref[rsl, vsl]
            y = o * lax.rsqrt(jnp.mean(o * o, axis=-1, keepdims=True) + RMS_EPS) * nw_ref[...]
            o_ref[rsl, vsl] = (y * (g * _sigmoid(g))).astype(o_ref.dtype)
        return carry

    if nchunks == 1:
        chunk_step(0, 0)
    else:
        lax.fori_loop(0, nchunks, chunk_step, 0)

    for hh in range(hb):
        so_ref[0, hh] = st_scr[hh].T


def _recurrence(y, extras, norm_w, s0, consts, *, mode, col_q, col_k, col_v, col_g, col_gz=None,
                row0, nseq, tb, t_valid, hb, name):
    _, h, dk, dv = s0.shape
    assert h % hb == 0 and row0 % tb == 0
    nv = min(tb, CHUNK) // SUBLANES
    nchunks = tb // (SUBLANES * nv)
    rb0 = row0 // tb
    kw, vw = hb * dk, hb * dv
    tri, masks = consts

    def cspec(width, col):
        assert col % width == 0
        cb = col // width
        return pl.BlockSpec((tb, width), lambda i, j: (rb0 + i, cb + j))

    fixh = lambda width: pl.BlockSpec((1, width), lambda i, j: (0, j))
    in_specs = [cspec(kw, col_q), cspec(kw, col_k), cspec(vw, col_v), cspec(vw, col_g)]
    args = [y, y, y, y]
    if mode == "gla":
        wgh, wgl, bg = extras
        in_specs += [pl.BlockSpec((tb, LANES), lambda i, j: (rb0 + i, col_gz // LANES)),
                     pl.BlockSpec((LANES, kw), lambda i, j: (0, j)),
                     pl.BlockSpec((LANES, kw), lambda i, j: (0, j)), fixh(kw)]
        args += [y, wgh, wgl, bg]
    else:
        in_specs += [fixh(kw), fixh(kw), fixh(kw)]
        args += list(extras)
    in_specs += [pl.BlockSpec((1, dv), lambda i, j: (0, 0)),
                 pl.BlockSpec((1, hb, dk, dv), lambda i, j: (i, j, 0, 0)),
                 pl.BlockSpec(tri.shape, lambda i, j: (0, 0)),
                 pl.BlockSpec(masks.shape, lambda i, j: (0, 0, 0))]
    args += [norm_w.reshape(1, dv), s0, tri, masks]
    return pl.pallas_call(
        functools.partial(_rec_kernel, mode=mode, hb=hb, dk=dk, dv=dv, nv=nv, nchunks=nchunks,
                          t_valid=t_valid, qscale=dk ** -0.5),
        grid=(nseq, h // hb),
        in_specs=in_specs,
        out_specs=[pl.BlockSpec((tb, vw), lambda i, j: (i, j)),
                   pl.BlockSpec((1, hb, dk, dv), lambda i, j: (i, j, 0, 0))],
        out_shape=[jax.ShapeDtypeStruct((nseq * tb, h * dv), BF16),
                   jax.ShapeDtypeStruct((nseq, h, dk, dv), F32)],
        scratch_shapes=[pltpu.VMEM((hb, dv, dk), F32)],
        compiler_params=_params("parallel", "parallel"),
        name=name,
    )(*args)


def _conv_kernel(bg_ref, cg_ref, h_ref, w_ref, buf_ref, y_ref, tail_ref, carry):
    t = pl.program_id(1)
    u = cg_ref[...] * h_ref[...]
    tt = u.shape[0]

    @pl.when(t == 0)
    def _():
        carry[...] = jnp.zeros(carry.shape, F32)
        carry[SUBLANES - 2:SUBLANES, :] = buf_ref[0]

    p6 = carry[SUBLANES - 2:SUBLANES - 1, :]
    p7 = carry[SUBLANES - 1:SUBLANES, :]
    row = lax.broadcasted_iota(jnp.int32, u.shape, 0)
    u1 = jnp.where(row == 0, p7, pltpu.roll(u, 1, axis=0))
    u2 = jnp.where(row == 0, p6, jnp.where(row == 1, p7, pltpu.roll(u, 2, axis=0)))
    y = bg_ref[...] * (u2 * w_ref[0:1, :] + u1 * w_ref[1:2, :] + u * w_ref[2:3, :])
    y_ref[...] = y.astype(y_ref.dtype)
    tail = u[tt - SUBLANES:tt, :]
    carry[...] = tail
    tail_ref[0] = tail


def _short_conv(y, w3, buf, *, col_b, col_c, col_h, width, row0, nseq, tb, name):
    tt = _pick_tile(tb, 512)
    nt = tb // tt
    rb0 = row0 // tt

    def cspec(col):
        cb = col // width
        return pl.BlockSpec((tt, width), lambda i, t: (rb0 + i * nt + t, cb))

    return pl.pallas_call(
        _conv_kernel,
        grid=(nseq, nt),
        in_specs=[cspec(col_b), cspec(col_c), cspec(col_h),
                  pl.BlockSpec((3, width), lambda i, t: (0, 0)),
                  pl.BlockSpec((1, 2, width), lambda i, t: (i, 0, 0))],
        out_specs=[pl.BlockSpec((tt, width), lambda i, t: (i * nt + t, 0)),
                   pl.BlockSpec((1, SUBLANES, width), lambda i, t: (i, 0, 0))],
        out_shape=[jax.ShapeDtypeStruct((nseq * tb, width), BF16),
                   jax.ShapeDtypeStruct((nseq, SUBLANES, width), F32)],
        scratch_shapes=[pltpu.VMEM((SUBLANES, width), F32)],
        compiler_params=_params("parallel", "arbitrary"),
        name=name,
    )(y, y, y, w3, buf)


def _attn_kernel(q_ref, k_ref, v_ref, o_ref, *, nb, rows, heads, hd):
    scale = hd ** -0.5
    m = max(rows, 2 * SUBLANES)
    for bi in range(nb):
        qb = _pad_rows(q_ref[bi * rows:(bi + 1) * rows, :], m)
        outs = []
        for hh in range(heads):
            sl = slice(hh * hd, (hh + 1) * hd)
            kh = k_ref[bi, :, sl].astype(BF16)
            vh = v_ref[bi, :, sl].astype(BF16)
            s = lax.dot_general(qb[:, sl].astype(BF16), kh, _NT, preferred_element_type=F32) * scale
            p = jnp.exp(s - jnp.max(s, axis=-1, keepdims=True))
            p = p / jnp.sum(p, axis=-1, keepdims=True)
            outs.append(_dot(p.astype(BF16), vh))
        o = jnp.concatenate(outs, axis=-1)
        o_ref[bi * rows:(bi + 1) * rows, :] = o[0:rows].astype(o_ref.dtype)


def _mem_attention(q, mem_k, mem_v, *, row0, nseq, tb, nb, heads, name):
    d = q.shape[1]
    mlen = mem_k.shape[1]
    hd = d // heads
    if nb > 1:
        rows, nt = tb, 1
    else:
        rows = _pick_tile(tb, 512)
        nt = tb // rows
    assert nseq % nb == 0 and row0 % (nb * rows) == 0
    rb0 = row0 // (nb * rows)
    return pl.pallas_call(
        functools.partial(_attn_kernel, nb=nb, rows=rows, heads=heads, hd=hd),
        grid=(nseq // nb, nt),
        in_specs=[pl.BlockSpec((nb * rows, d), lambda i, t: (rb0 + i * nt + t, 0)),
                  pl.BlockSpec((nb, mlen, d), lambda i, t: (i, 0, 0)),
                  pl.BlockSpec((nb, mlen, d), lambda i, t: (i, 0, 0))],
        out_specs=pl.BlockSpec((nb * rows, d), lambda i, t: (i * nt + t, 0)),
        out_shape=jax.ShapeDtypeStruct((nseq * tb, d), BF16),
        compiler_params=_params("parallel", "parallel"),
        name=name,
    )(q, mem_k, mem_v)


def _gather_kernel(idx_ref, src_ref, dst_ref, sem, *, rows):
    base = pl.program_id(0) * rows

    def row_copy(src_row, dst_row):
        return pltpu.make_async_copy(src_ref.at[pl.ds(src_row, 1)], dst_ref.at[pl.ds(dst_row, 1)], sem)

    def start(r, c):
        row_copy(idx_ref[r], base + r).start()
        return c

    lax.fori_loop(0, rows, start, 0)

    def wait(r, c):
        row_copy(0, base + r).wait()
        return c

    lax.fori_loop(0, rows, wait, 0)


def _gather_tiles(src, idx, *, name):
    p = idx.shape[0]
    rows = _pick_tile(p, 256)
    return pl.pallas_call(
        functools.partial(_gather_kernel, rows=rows),
        grid=(p // rows,),
        in_specs=[pl.BlockSpec((rows,), lambda i: (i,), memory_space=pltpu.SMEM),
                  pl.BlockSpec(memory_space=pl.ANY)],
        out_specs=pl.BlockSpec(memory_space=pl.ANY),
        out_shape=jax.ShapeDtypeStruct((p,) + src.shape[1:], src.dtype),
        scratch_shapes=[pltpu.SemaphoreType.DMA(())],
        compiler_params=_params("arbitrary"),
        name=name,
    )(idx, src)


def _expert_kernel(te_ref, nu_ref, x_ref, wg_ref, wu_ref, wd_ref, o_ref, *, d):
    @pl.when(pl.program_id(0) < nu_ref[0])
    def _():
        x = _load_token_tiles(x_ref, EXPERT_TILE, d).astype(BF16)
        gate = _dot(x, wg_ref[0])
        up = _dot(x, wu_ref[0])
        hid = (gate * _sigmoid(gate) * up).astype(BF16)
        _store_token_tiles(o_ref, _dot(hid, wd_ref[0]))

    @pl.when(pl.program_id(0) >= nu_ref[0])
    def _():
        o_ref[...] = jnp.zeros(o_ref.shape, F32)


def _experts(xs, tile_expert, n_used, wg, wu, wd):
    d, f = wg.shape[1], wg.shape[2]
    nsub = d // LANES
    nt = xs.shape[0] // (EXPERT_TILE * nsub)
    live = lambda i, te, nu: jnp.minimum(i, nu[0] - 1)
    blk = (EXPERT_TILE * nsub, LANES)
    grid_spec = pltpu.PrefetchScalarGridSpec(
        num_scalar_prefetch=2,
        grid=(nt,),
        in_specs=[pl.BlockSpec(blk, lambda i, te, nu: (live(i, te, nu), 0)),
                  pl.BlockSpec((1, d, f), lambda i, te, nu: (te[live(i, te, nu)], 0, 0)),
                  pl.BlockSpec((1, d, f), lambda i, te, nu: (te[live(i, te, nu)], 0, 0)),
                  pl.BlockSpec((1, f, d), lambda i, te, nu: (te[live(i, te, nu)], 0, 0))],
        out_specs=pl.BlockSpec(blk, lambda i, te, nu: (i, 0)),
    )
    return pl.pallas_call(
        functools.partial(_expert_kernel, d=d),
        grid_spec=grid_spec,
        out_shape=jax.ShapeDtypeStruct(xs.shape, F32),
        compiler_params=_params("arbitrary"),
        name="moe_experts",
    )(tile_expert, n_used, xs, wg, wu, wd)


def _combine_ln_kernel(ya_ref, yb_ref, r_ref, x_ref, g_ref, b_ref, xf_ref, xb_ref, *, alpha):
    rows, d = x_ref.shape
    w0 = r_ref[:, 0:1]
    w1 = r_ref[:, 1:2]
    moe = w0 * _load_token_tiles(ya_ref, rows, d) + w1 * _load_token_tiles(yb_ref, rows, d)
    y = _layer_norm_rows(alpha * x_ref[...] + moe, g_ref[...], b_ref[...])
    xf_ref[...] = y
    xb_ref[...] = y.astype(BF16)


def _combine_ln(ya, yb, rinfo, x, g, b, *, alpha, name):
    n, d = x.shape
    tm = _pick_tile(n, 512)
    row = lambda i: (i, 0)
    fix = lambda i: (0, 0)
    tile_blk = pl.BlockSpec((tm * (d // LANES), LANES), row)
    return pl.pallas_call(
        functools.partial(_combine_ln_kernel, alpha=alpha),
        grid=(n // tm,),
        in_specs=[tile_blk, tile_blk, pl.BlockSpec((tm, LANES), row), pl.BlockSpec((tm, d), row),
                  pl.BlockSpec((1, d), fix), pl.BlockSpec((1, d), fix)],
        out_specs=[pl.BlockSpec((tm, d), row), pl.BlockSpec((tm, d), row)],
        out_shape=[jax.ShapeDtypeStruct((n, d), F32), jax.ShapeDtypeStruct((n, d), BF16)],
        compiler_params=_params("parallel"),
        name=name,
    )(ya, yb, rinfo, x, g.reshape(1, d), b.reshape(1, d))


def _dispatch_plan(rinfo, n_exp_total):
    n = rinfo.shape[0]
    a = n * TOP_K
    eid = rinfo[:, 2:2 + TOP_K].astype(jnp.int32).reshape(a)
    onehot = (eid[:, None] == jnp.arange(n_exp_total, dtype=jnp.int32)[None, :]).astype(jnp.int32)
    csum = jnp.cumsum(onehot, axis=0)
    rank = jnp.sum(csum * onehot, axis=1) - 1
    counts = csum[-1]
    padded = ((counts + EXPERT_TILE - 1) // EXPERT_TILE) * EXPERT_TILE
    ends = jnp.cumsum(padded)
    starts = ends - padded
    dest = starts[eid] + rank
    nt = -(-a // EXPERT_TILE) + n_exp_total
    p = nt * EXPERT_TILE
    src_tok = jnp.zeros((p,), jnp.int32).at[dest].set(jnp.arange(a, dtype=jnp.int32) // TOP_K)
    tile_start = jnp.arange(nt, dtype=jnp.int32) * EXPERT_TILE
    tile_expert = jnp.minimum(jnp.sum(tile_start[:, None] >= ends[None, :], axis=1), n_exp_total - 1).astype(jnp.int32)
    n_used = (ends[-1] // EXPERT_TILE).astype(jnp.int32).reshape(1)
    return dest.astype(jnp.int32), src_tok, tile_expert, n_used


def kernel(x_prompt, x_sample, state_hgrn, state_conv, state_gla, cache_mem_k, cache_mem_v, mem_prompt, w_in_a, lb_a, conv_w, gnorm_a, w_out_a, w_in_c, w_gk2, b_gk2, gnorm_c, w_out_c, w_mq, w_mk, w_mv, w_mo, ln_g, ln_b, w_grp, b_grp, w_er, b_er, w_gate, w_up, w_down):
    bsz, seq, d = x_prompt.shape
    dec_b, dec_t, _ = x_sample.shape
    depth = w_mq.shape[0]
    _, _, ha_h, ha_dk, ha_dv = state_hgrn.shape
    _, _, gl_h, gl_dk, gl_dv = state_gla.shape
    _, _, mlen, mem_h, mem_hd = cache_mem_k.shape
    n_groups, n_exp = w_er.shape[1], w_er.shape[3]
    sc_w = conv_w.shape[1]
    sc_k = conv_w.shape[2]
    ha_f, ha_w = ha_h * ha_dk, ha_h * ha_dv
    gl_key, gl_val = gl_h * gl_dk, gl_h * gl_dv
    rank = w_gk2.shape[1]
    alpha = (2.0 * depth) ** 0.25
    assert dec_t <= SAMPLE_ROWS and sc_k == 3 and seq % CHUNK == 0 and rank <= LANES
    assert n_groups + n_groups * n_exp <= LANES

    rows_p = bsz * seq
    rows_s = dec_b * SAMPLE_ROWS
    xs_pad = jnp.pad(x_sample, ((0, 0), (0, SAMPLE_ROWS - dec_t), (0, 0)))
    x = jnp.concatenate([x_prompt.reshape(rows_p, d), xs_pad.reshape(rows_s, d)], axis=0)
    xb = x.astype(BF16)
    consts = _rec_constants()

    kv_w = jnp.concatenate([w_mk, w_mv], axis=0).astype(BF16)
    kv = _mem_kv(mem_prompt.reshape(bsz * mlen, d).astype(BF16), kv_w)
    mk_p = kv[:depth].reshape(depth, bsz, mlen, d)
    mv_p = kv[depth:].reshape(depth, bsz, mlen, d)

    lb = jnp.cumsum(jax.nn.softmax(lb_a.astype(F32), axis=0), axis=0)
    lb = lb - lb[0]
    log_lb, log1m_lb, onem_lb = jnp.log(lb), jnp.log1p(-lb), 1.0 - lb

    z_hgrn = jnp.zeros((bsz, ha_h, ha_dk, ha_dv), F32)
    z_conv = jnp.zeros((bsz, sc_k - 1, sc_w), F32)
    z_gla = jnp.zeros((bsz, gl_h, gl_dk, gl_dv), F32)

    hgrn_p, conv_p, gla_p, hgrn_s, conv_s, gla_s = [], [], [], [], [], []
    for l in range(depth):
        j = l // 2
        if l % 2 == 0:
            y = _matmul(xb, w_in_a[j].astype(BF16), tn=512, name=f"in_proj_{l}")
            extras = (log_lb[j:j + 1], log1m_lb[j:j + 1], onem_lb[j:j + 1])
            cols = dict(mode="hgrn", col_q=0, col_k=ha_f, col_v=2 * ha_f, col_g=2 * ha_f + ha_w)
            o_p, s_p = _recurrence(y, extras, gnorm_a[j], z_hgrn, consts, row0=0, nseq=bsz, tb=seq,
                                   t_valid=seq, hb=1, name=f"hgrn_prompt_{l}", **cols)
            o_s, s_s = _recurrence(y, extras, gnorm_a[j], state_hgrn[j], consts, row0=rows_p, nseq=dec_b,
                                   tb=SAMPLE_ROWS, t_valid=dec_t, hb=ha_h, name=f"hgrn_sample_{l}", **cols)
            c0 = 2 * ha_f + 2 * ha_w
            w3 = conv_w[j].T
            ccols = dict(col_b=c0, col_c=c0 + sc_w, col_h=c0 + 2 * sc_w, width=sc_w)
            c_p, t_p = _short_conv(y, w3, z_conv, row0=0, nseq=bsz, tb=seq, name=f"conv_prompt_{l}", **ccols)
            c_s, t_s = _short_conv(y, w3, state_conv[j], row0=rows_p, nseq=dec_b, tb=SAMPLE_ROWS,
                                   name=f"conv_sample_{l}", **ccols)
            hgrn_p.append(s_p)
            hgrn_s.append(s_s)
            conv_p.append(t_p[:, SUBLANES - (sc_k - 1):])
            conv_s.append(t_s[:, dec_t - (sc_k - 1):dec_t])
            acts = [jnp.concatenate([o_p, o_s], axis=0), jnp.concatenate([c_p, c_s], axis=0)]
            wo = w_out_a[j].astype(BF16)
            weights = [wo[:ha_w], wo[ha_w:]]
        else:
            ncol = 2 * gl_key + 2 * gl_val
            w_in = jnp.pad(w_in_c[j], ((0, 0), (0, LANES - rank))).astype(BF16)
            y = _matmul(xb, w_in, tn=640 if (ncol + LANES) % 640 == 0 else LANES, name=f"in_proj_{l}")
            wg2 = jnp.pad(w_gk2[j], ((0, LANES - rank), (0, 0)))
            wgh = wg2.astype(BF16)
            wgl = (wg2 - wgh.astype(F32)).astype(BF16)
            extras = (wgh, wgl, b_gk2[j].reshape(1, gl_key))
            cols = dict(mode="gla", col_q=0, col_k=gl_key, col_v=2 * gl_key, col_g=2 * gl_key + gl_val, col_gz=ncol)
            o_p, s_p = _recurrence(y, extras, gnorm_c[j], z_gla, consts, row0=0, nseq=bsz, tb=seq,
                                   t_valid=seq, hb=1, name=f"gla_prompt_{l}", **cols)
            o_s, s_s = _recurrence(y, extras, gnorm_c[j], state_gla[j], consts, row0=rows_p, nseq=dec_b,
                                   tb=SAMPLE_ROWS, t_valid=dec_t, hb=gl_h, name=f"gla_sample_{l}", **cols)
            gla_p.append(s_p)
            gla_s.append(s_s)
            acts = [jnp.concatenate([o_p, o_s], axis=0)]
            weights = [w_out_c[j].astype(BF16)]
        x, xb = _proj_ln(acts, weights, x, ln_g[l, 0], ln_b[l, 0], alpha=alpha, name=f"mix_out_ln_{l}")

        q = _matmul(xb, w_mq[l].astype(BF16), tn=512, name=f"mem_q_{l}")
        a_p = _mem_attention(q, mk_p[l], mv_p[l], row0=0, nseq=bsz, tb=seq, nb=1, heads=mem_h, name=f"mem_attn_prompt_{l}")
        a_s = _mem_attention(q, cache_mem_k[l].reshape(dec_b, mlen, d), cache_mem_v[l].reshape(dec_b, mlen, d),
                             row0=rows_p, nseq=dec_b, tb=SAMPLE_ROWS, nb=4 if dec_b % 4 == 0 else 1, heads=mem_h,
                             name=f"mem_attn_sample_{l}")
        wr = jnp.concatenate([w_grp[l]] + [w_er[l, gi] for gi in range(n_groups)], axis=1)
        wr = jnp.pad(wr, ((0, 0), (0, LANES - wr.shape[1])))
        wrh = wr.astype(BF16)
        wrl = (wr - wrh.astype(F32)).astype(BF16)
        br = jnp.concatenate([b_grp[l], b_er[l].reshape(-1)])
        br = jnp.pad(br, (0, LANES - br.shape[0])).reshape(1, LANES)
        x, xb, rinfo, xt = _proj_ln([jnp.concatenate([a_p, a_s], axis=0)], [w_mo[l].astype(BF16)], x, ln_g[l, 1],
                                    ln_b[l, 1], alpha=alpha, router_w=(wrh, wrl, br), router=(n_groups, n_exp),
                                    name=f"mem_out_ln_router_{l}")

        ne = n_groups * n_exp
        nsub = d // LANES
        dest, src_tok, tile_expert, n_used = _dispatch_plan(rinfo, ne)
        xs = _gather_tiles(xt.reshape(-1, nsub, LANES), src_tok, name=f"moe_gather_{l}")
        f = w_gate.shape[-1]
        ys = _experts(xs.reshape(-1, LANES), tile_expert, n_used, w_gate[l].reshape(ne, d, f).astype(BF16),
                      w_up[l].reshape(ne, d, f).astype(BF16), w_down[l].reshape(ne, f, d).astype(BF16))
        ys = ys.reshape(-1, nsub, LANES)
        ya = _gather_tiles(ys, dest[0::TOP_K], name=f"moe_unpermute_a_{l}")
        yb = _gather_tiles(ys, dest[1::TOP_K], name=f"moe_unpermute_b_{l}")
        x, xb = _combine_ln(ya.reshape(-1, LANES), yb.reshape(-1, LANES), rinfo, x, ln_g[l, 2], ln_b[l, 2],
                            alpha=alpha, name=f"moe_combine_ln_{l}")

    y_prompt = x[:rows_p].reshape(bsz, seq, d)
    y_sample = x[rows_p:].reshape(dec_b, SAMPLE_ROWS, d)[:, :dec_t]
    return (y_prompt, y_sample, jnp.stack(hgrn_p), jnp.stack(conv_p), jnp.stack(gla_p),
            mk_p.reshape(depth, bsz, mlen, mem_h, mem_hd), mv_p.reshape(depth, bsz, mlen, mem_h, mem_hd),
            jnp.stack(hgrn_s), jnp.stack(conv_s), jnp.stack(gla_s))
```

```python
import functools
import math

import numpy as np
import jax
import jax.numpy as jnp
from jax import lax
from jax.experimental import pallas as pl
from jax.experimental.pallas import tpu as pltpu

F32 = jnp.float32
BF16 = jnp.bfloat16

LN_EPS = 1e-5
RMS_EPS = 1e-6
GLA_NORMALIZER = 16.0
TOP_K = 2

LANES = 128
SUBLANES = 8
CHUNK = 128
SAMPLE_ROWS = 8
EXPERT_TILE = 256
VMEM_LIMIT = 48 * 1024 * 1024

_NT = (((1,), (1,)), ((), ()))
_TN = (((0,), (0,)), ((), ()))


def _params(*sem):
    return pltpu.CompilerParams(dimension_semantics=sem, vmem_limit_bytes=VMEM_LIMIT)


def _pick_tile(n, pref):
    t = pref
    while t > SUBLANES and n % t:
        t //= 2
    assert n % t == 0, (n, pref)
    return t


def _dot(a, b):
    return jnp.dot(a, b, preferred_element_type=F32)


def _sigmoid(x):
    return 1.0 / (1.0 + jnp.exp(-x))


def _log_sigmoid(x):
    return jnp.minimum(x, 0.0) - jnp.log1p(jnp.exp(-jnp.abs(x)))


def _split2(a):
    hi = a.astype(BF16)
    lo = (a - hi.astype(F32)).astype(BF16)
    return hi, lo


def _split3(a):
    hi = a.astype(BF16)
    r = a - hi.astype(F32)
    mid = r.astype(BF16)
    lo = (r - mid.astype(F32)).astype(BF16)
    return hi, mid, lo


def _pad_rows(x, n):
    if x.shape[0] == n:
        return x
    return jnp.concatenate([x, jnp.zeros((n - x.shape[0], x.shape[1]), x.dtype)], axis=0)


def _store_token_tiles(ref, y):
    rows, d = y.shape
    nsub = d // LANES
    for s in range(nsub):
        ref[pl.ds(s, rows, stride=nsub), :] = y[:, s * LANES:(s + 1) * LANES]


def _load_token_tiles(ref, rows, d):
    nsub = d // LANES
    return jnp.concatenate([ref[pl.ds(s, rows, stride=nsub), :] for s in range(nsub)], axis=1)


def _mm_kernel(x_ref, w_ref, o_ref):
    o_ref[...] = _dot(x_ref[...], w_ref[...]).astype(o_ref.dtype)


def _matmul(x, w, *, tn, out_dtype=F32, name):
    n, k = x.shape
    m = w.shape[1]
    tm = _pick_tile(n, 512)
    return pl.pallas_call(
        _mm_kernel,
        grid=(n // tm, m // tn),
        in_specs=[pl.BlockSpec((tm, k), lambda i, j: (i, 0)),
                  pl.BlockSpec((k, tn), lambda i, j: (0, j))],
        out_specs=pl.BlockSpec((tm, tn), lambda i, j: (i, j)),
        out_shape=jax.ShapeDtypeStruct((n, m), out_dtype),
        compiler_params=_params("parallel", "parallel"),
        name=name,
    )(x, w)


def _mem_kv_kernel(x_ref, w_ref, o_ref, *, heads):
    y = _dot(x_ref[0], w_ref[0])
    hd = y.shape[1] // heads
    for hh in range(heads):
        o_ref[0, :, hh, :] = y[:, hh * hd:(hh + 1) * hd]


def _mem_kv(mem_bf16, w_all, heads, name):
    nb, mlen, d = mem_bf16.shape
    nj, _, dout = w_all.shape
    return pl.pallas_call(
        functools.partial(_mem_kv_kernel, heads=heads),
        grid=(nj, nb),
        in_specs=[pl.BlockSpec((1, mlen, d), lambda j, i: (i, 0, 0)),
                  pl.BlockSpec((1, d, dout), lambda j, i: (j, 0, 0))],
        out_specs=pl.BlockSpec((1, mlen, heads, dout // heads), lambda j, i: (j * nb + i, 0, 0, 0)),
        out_shape=jax.ShapeDtypeStruct((nj * nb, mlen, heads, dout // heads), F32),
        compiler_params=_params("parallel", "parallel"),
        name=name,
    )(mem_bf16, w_all)


def _layer_norm_rows(z, g, b):
    mu = jnp.mean(z, axis=-1, keepdims=True)
    zc = z - mu
    var = jnp.mean(zc * zc, axis=-1, keepdims=True)
    return zc * lax.rsqrt(var + LN_EPS) * g + b


def _route(logits, n_groups, n_exp):
    lane = lax.broadcasted_iota(jnp.int32, logits.shape, 1).astype(F32)
    big = jnp.float32(1 << 20)
    neg = jnp.float32(-jnp.inf)
    gl = jnp.where(lane < n_groups, logits, neg)
    gmax = jnp.max(gl, axis=-1, keepdims=True)
    gsel = jnp.min(jnp.where(gl == gmax, lane, big), axis=-1, keepdims=True)
    gden = jnp.sum(jnp.exp(gl - gmax), axis=-1, keepdims=True)
    g_w = 1.0 / gden
    lo = n_groups + gsel * n_exp
    el = jnp.where((lane >= lo) & (lane < lo + n_exp), logits, neg)
    m1 = jnp.max(el, axis=-1, keepdims=True)
    i1 = jnp.min(jnp.where(el == m1, lane, big), axis=-1, keepdims=True)
    el2 = jnp.where(lane == i1, neg, el)
    m2 = jnp.max(el2, axis=-1, keepdims=True)
    i2 = jnp.min(jnp.where(el2 == m2, lane, big), axis=-1, keepdims=True)
    t = jnp.exp(m2 - m1)
    w1 = g_w / (1.0 + t)
    w2 = g_w * t / (1.0 + t)
    e1 = i1 - n_groups
    e2 = i2 - n_groups
    zero = jnp.zeros(logits.shape, F32)
    return jnp.where(lane == 0, w1, jnp.where(lane == 1, w2, jnp.where(lane == 2, e1, jnp.where(lane == 3, e2, zero))))


def _proj_ln_kernel(*refs, n_in, alpha, router, prompt_tiles):
    ap_refs = refs[:n_in]
    as_refs = refs[n_in:2 * n_in]
    w_refs = refs[2 * n_in:3 * n_in]
    x_ref, g_ref, b_ref = refs[3 * n_in:3 * n_in + 3]
    pos = 3 * n_in + 3
    if router is not None:
        wrh_ref, wrl_ref, br_ref = refs[pos:pos + 3]
        pos += 3
    xf_ref, xb_ref = refs[pos:pos + 2]
    is_prompt = pl.program_id(0) < prompt_tiles
    acc = None
    for ap_ref, as_ref, w_ref in zip(ap_refs, as_refs, w_refs):
        part = _dot(jnp.where(is_prompt, ap_ref[...], as_ref[...]), w_ref[...])
        acc = part if acc is None else acc + part
    y = _layer_norm_rows(alpha * x_ref[...] + acc, g_ref[...], b_ref[...])
    xf_ref[...] = y
    xb_ref[...] = y.astype(BF16)
    if router is not None:
        hi, lo = _split2(y)
        logits = _dot(hi, wrh_ref[...]) + _dot(lo, wrh_ref[...]) + _dot(hi, wrl_ref[...]) + br_ref[...]
        refs[pos + 2][...] = _route(logits, *router)
        _store_token_tiles(refs[pos + 3], y)


def _proj_ln(acts, weights, x, g, b, *, alpha, router_w=None, router=None, name):
    n, d = x.shape
    n_p = acts[0][0].shape[0]
    tm = _pick_tile(math.gcd(n_p, n - n_p), 512)
    npt = n_p // tm
    row = lambda i: (i, 0)
    fix = lambda i: (0, 0)
    in_specs = [pl.BlockSpec((tm, a[0].shape[1]), lambda i: (jnp.minimum(i, npt - 1), 0)) for a in acts]
    in_specs += [pl.BlockSpec((tm, a[1].shape[1]), lambda i: (jnp.maximum(i - npt, 0), 0)) for a in acts]
    in_specs += [pl.BlockSpec(w.shape, fix) for w in weights]
    in_specs += [pl.BlockSpec((tm, d), row), pl.BlockSpec((1, d), fix), pl.BlockSpec((1, d), fix)]
    args = [a[0] for a in acts] + [a[1] for a in acts] + list(weights) + [x, g.reshape(1, d), b.reshape(1, d)]
    out_specs = [pl.BlockSpec((tm, d), row), pl.BlockSpec((tm, d), row)]
    out_shape = [jax.ShapeDtypeStruct((n, d), F32), jax.ShapeDtypeStruct((n, d), BF16)]
    if router_w is not None:
        in_specs += [pl.BlockSpec(r.shape, fix) for r in router_w]
        args += list(router_w)
        out_specs.append(pl.BlockSpec((tm, LANES), row))
        out_shape.append(jax.ShapeDtypeStruct((n, LANES), F32))
        out_specs.append(pl.BlockSpec((tm * (d // LANES), LANES), row))
        out_shape.append(jax.ShapeDtypeStruct((n * (d // LANES), LANES), F32))
    return pl.pallas_call(
        functools.partial(_proj_ln_kernel, n_in=len(acts), alpha=alpha, router=router, prompt_tiles=npt),
        grid=(n // tm,),
        in_specs=in_specs, out_specs=out_specs, out_shape=out_shape,
        compiler_params=_params("parallel"),
        name=name,
    )(*args)


_LEVELS = (8, 16, 32, 64)


def _rec_constants():
    t = np.arange(CHUNK)
    tri = (t[:, None] >= t[None, :]).astype(np.float32)
    masks = []
    for lvl in _LEVELS:
        bt, bs = t[:, None] // lvl, t[None, :] // lvl
        masks.append(((bt - bs == 1) & (bt % 2 == 1)).astype(np.float32))
    return jnp.asarray(tri, BF16), jnp.asarray(np.stack(masks), F32)


def _rec_chunk(q, k, a, v, st, tri_ref, mask_ref, nv):
    r = SUBLANES * nv
    m = max(r, 2 * SUBLANES)
    dk = q.shape[1]

    hi, mid, lo = _split3(_pad_rows(a, CHUNK))
    tri = tri_ref[0:m, :]
    b = (_dot(tri, hi) + _dot(tri, mid) + _dot(tri, lo))[0:r]
    b_last = b[r - 1:r]

    scores = None
    for li, lvl in enumerate(_LEVELS):
        if lvl >= r:
            continue
        qp, kp = [], []
        for blk in range(r // lvl):
            rows = slice(blk * lvl, (blk + 1) * lvl)
            if blk % 2 == 1:
                rho = b[blk * lvl - 1:blk * lvl]
                qp.append(q[rows] * jnp.exp(b[rows] - rho))
                kp.append(jnp.zeros((lvl, dk), F32))
            else:
                rho = b[(blk + 1) * lvl - 1:(blk + 1) * lvl]
                kp.append(k[rows] * jnp.exp(rho - b[rows]))
                qp.append(jnp.zeros((lvl, dk), F32))
        qh = jnp.concatenate(qp, axis=0).astype(BF16)
        kh = jnp.concatenate(kp, axis=0).astype(BF16)
        s = lax.dot_general(qh, kh, _NT, preferred_element_type=F32) * mask_ref[li]
        scores = s if scores is None else scores + s

    lane = lax.broadcasted_iota(jnp.int32, (SUBLANES, CHUNK), 1)
    sub = lax.broadcasted_iota(jnp.int32, (SUBLANES, CHUNK), 0)
    dparts = []
    for j in range(nv):
        rows = slice(SUBLANES * j, SUBLANES * (j + 1))
        qj, kj, bj = q[rows], k[rows], b[rows]
        acc = jnp.zeros((SUBLANES, CHUNK), F32)
        for s_ in range(SUBLANES):
            e = jnp.exp(jnp.minimum(bj - bj[s_:s_ + 1], 0.0))
            col = jnp.sum(qj * kj[s_:s_ + 1] * e, axis=1, keepdims=True)
            acc = jnp.where(lane == SUBLANES * j + s_, col, acc)
        dparts.append(jnp.where(lane <= sub + SUBLANES * j, acc, 0.0))
    diag = _pad_rows(jnp.concatenate(dparts, axis=0) if nv > 1 else dparts[0], m)
    scores = diag if scores is None else scores + diag

    v_bf = _pad_rows(v, CHUNK).astype(BF16)
    o = _dot(scores.astype(BF16), v_bf)
    qe = _pad_rows(q * jnp.exp(b), m).astype(BF16)
    o = o + lax.dot_general(qe, st.astype(BF16), _NT, preferred_element_type=F32)
    kd = _pad_rows(k * jnp.exp(b_last - b), CHUNK).astype(BF16)
    st_new = st * jnp.exp(b_last) + lax.dot_general(v_bf, kd, _TN, preferred_element_type=F32)
    return o[0:r], st_new


def _rec_kernel(*refs, mode, hb, dk, dv, nv, nchunks, t_valid, qscale):
    q_ref, k_ref, v_ref, g_ref = refs[:4]
    pos = 4
    if mode == "gla":
        gz_ref, wgh_ref, wgl_ref, bg_ref = refs[pos:pos + 4]
        pos += 4
    else:
        loglb_ref, log1m_ref, onem_ref = refs[pos:pos + 3]
        pos += 3
    nw_ref, s0_ref, tri_ref, mask_ref = refs[pos:pos + 4]
    o_ref, so_ref, st_scr = refs[pos + 4:pos + 7]
    r = SUBLANES * nv

    for hh in range(hb):
        st_scr[hh] = s0_ref[0, hh].T

    def chunk_step(c, carry):
        rsl = pl.ds(c * r if isinstance(c, int) else pl.multiple_of(c * r, r), r)
        if t_valid < r:
            valid_k = lax.broadcasted_iota(jnp.int32, (r, dk), 0) < t_valid
            valid_v = lax.broadcasted_iota(jnp.int32, (r, dv), 0) < t_valid
        if mode == "gla":
            gz = _pad_rows(gz_ref[rsl, :], max(r, 2 * SUBLANES))
            gzh, gzl = _split2(gz)
        for hh in range(hb):
            ksl = slice(hh * dk, (hh + 1) * dk)
            vsl = slice(hh * dv, (hh + 1) * dv)
            qr = q_ref[rsl, ksl]
            kr = k_ref[rsl, ksl]
            v = v_ref[rsl, vsl]
            if mode == "gla":
                gk = (_dot(gzh, wgh_ref[:, ksl]) + _dot(gzl, wgh_ref[:, ksl]) + _dot(gzh, wgl_ref[:, ksl]))[0:r]
                a = _log_sigmoid(gk + bg_ref[:, ksl]) / GLA_NORMALIZER
                q = qr * qscale
                k = kr
            else:
                ls = _log_sigmoid(kr)
                x2 = log1m_ref[:, ksl] + ls
                x1 = loglb_ref[:, ksl]
                a = jnp.maximum(x1, x2) + jnp.log1p(jnp.exp(-jnp.abs(x1 - x2)))
                k = onem_ref[:, ksl] * _sigmoid(-kr)
                q = qr * _sigmoid(qr)
            if t_valid < r:
                a = jnp.where(valid_k, a, 0.0)
                k = jnp.where(valid_k, k, 0.0)
                v = jnp.where(valid_v, v, 0.0)
            o, st_new = _rec_chunk(q, k, a, v, st_scr[hh], tri_ref, mask_ref, nv)
            st_scr[hh] = st_new
            g = g_ref[rsl, vsl]
            y = o * lax.rsqrt(jnp.mean(o * o, axis=-1, keepdims=True) + RMS_EPS) * nw_ref[...]
            o_ref[rsl, vsl] = (y * (g * _sigmoid(g))).astype(o_ref.dtype)
        return carry

    if nchunks == 1:
        chunk_step(0, 0)
    else:
        lax.fori_loop(0, nchunks, chunk_step, 0)

    for hh in range(hb):
        so_ref[0, hh] = st_scr[hh].T


def _recurrence(y, extras, norm_w, s0, consts, *, mode, col_q, col_k, col_v, col_g, col_gz=None,
                row0, nseq, tb, t_valid, hb, name, s_off=0):
    _, h, dk, dv = s0.shape
    assert h % hb == 0 and row0 % tb == 0
    nv = min(tb, CHUNK) // SUBLANES
    nchunks = tb // (SUBLANES * nv)
    rb0 = row0 // tb
    kw, vw = hb * dk, hb * dv
    tri, masks = consts

    def cspec(width, col):
        assert col % width == 0
        cb = col // width
        return pl.BlockSpec((tb, width), lambda i, j: (rb0 + i, cb + j))

    fixh = lambda width: pl.BlockSpec((1, width), lambda i, j: (0, j))
    in_specs = [cspec(kw, col_q), cspec(kw, col_k), cspec(vw, col_v), cspec(vw, col_g)]
    args = [y, y, y, y]
    if mode == "gla":
        wgh, wgl, bg = extras
        in_specs += [pl.BlockSpec((tb, LANES), lambda i, j: (rb0 + i, col_gz // LANES)),
                     pl.BlockSpec((LANES, kw), lambda i, j: (0, j)),
                     pl.BlockSpec((LANES, kw), lambda i, j: (0, j)), fixh(kw)]
        args += [y, wgh, wgl, bg]
    else:
        in_specs += [fixh(kw), fixh(kw), fixh(kw)]
        args += list(extras)
    in_specs += [pl.BlockSpec((1, dv), lambda i, j: (0, 0)),
                 pl.BlockSpec((1, hb, dk, dv), lambda i, j: (s_off + i, j, 0, 0)),
                 pl.BlockSpec(tri.shape, lambda i, j: (0, 0)),
                 pl.BlockSpec(masks.shape, lambda i, j: (0, 0, 0))]
    args += [norm_w.reshape(1, dv), s0, tri, masks]
    return pl.pallas_call(
        functools.partial(_rec_kernel, mode=mode, hb=hb, dk=dk, dv=dv, nv=nv, nchunks=nchunks,
                          t_valid=t_valid, qscale=dk ** -0.5),
        grid=(nseq, h // hb),
        in_specs=in_specs,
        out_specs=[pl.BlockSpec((tb, vw), lambda i, j: (i, j)),
                   pl.BlockSpec((1, hb, dk, dv), lambda i, j: (i, j, 0, 0))],
        out_shape=[jax.ShapeDtypeStruct((nseq * tb, h * dv), BF16),
                   jax.ShapeDtypeStruct((nseq, h, dk, dv), F32)],
        scratch_shapes=[pltpu.VMEM((hb, dv, dk), F32)],
        compiler_params=_params("parallel", "parallel"),
        name=name,
    )(*args)


def _conv_kernel(bg_ref, cg_ref, h_ref, w_ref, buf_ref, y_ref, tail_ref, carry):
    t = pl.program_id(1)
    u = cg_ref[...] * h_ref[...]
    tt = u.shape[0]

    @pl.when(t == 0)
    def _():
        carry[...] = jnp.zeros(carry.shape, F32)
        carry[SUBLANES - 2:SUBLANES, :] = buf_ref[0]

    p6 = carry[SUBLANES - 2:SUBLANES - 1, :]
    p7 = carry[SUBLANES - 1:SUBLANES, :]
    row = lax.broadcasted_iota(jnp.int32, u.shape, 0)
    u1 = jnp.where(row == 0, p7, pltpu.roll(u, 1, axis=0))
    u2 = jnp.where(row == 0, p6, jnp.where(row == 1, p7, pltpu.roll(u, 2, axis=0)))
    y = bg_ref[...] * (u2 * w_ref[0:1, :] + u1 * w_ref[1:2, :] + u * w_ref[2:3, :])
    y_ref[...] = y.astype(y_ref.dtype)
    tail = u[tt - SUBLANES:tt, :]
    carry[...] = tail
    tail_ref[0] = tail


def _short_conv(y, w3, buf, *, col_b, col_c, col_h, width, row0, nseq, tb, name):
    tt = _pick_tile(tb, 512)
    nt = tb // tt
    rb0 = row0 // tt

    def cspec(col):
        cb = col // width
        return pl.BlockSpec((tt, width), lambda i, t: (rb0 + i * nt + t, cb))

    return pl.pallas_call(
        _conv_kernel,
        grid=(nseq, nt),
        in_specs=[cspec(col_b), cspec(col_c), cspec(col_h),
                  pl.BlockSpec((3, width), lambda i, t: (0, 0)),
                  pl.BlockSpec((1, 2, width), lambda i, t: (i, 0, 0))],
        out_specs=[pl.BlockSpec((tt, width), lambda i, t: (i * nt + t, 0)),
                   pl.BlockSpec((1, SUBLANES, width), lambda i, t: (i, 0, 0))],
        out_shape=[jax.ShapeDtypeStruct((nseq * tb, width), BF16),
                   jax.ShapeDtypeStruct((nseq, SUBLANES, width), F32)],
        scratch_shapes=[pltpu.VMEM((SUBLANES, width), F32)],
        compiler_params=_params("parallel", "arbitrary"),
        name=name,
    )(y, y, y, w3, buf)


def _attn_kernel(q_ref, k_ref, v_ref, o_ref, *, nb, rows, heads, hd):
    scale = hd ** -0.5
    m = max(rows, 2 * SUBLANES)
    for bi in range(nb):
        qb = _pad_rows(q_ref[bi * rows:(bi + 1) * rows, :], m)
        outs = []
        for hh in range(heads):
            sl = slice(hh * hd, (hh + 1) * hd)
            kh = k_ref[bi, :, hh, :].astype(BF16)
            vh = v_ref[bi, :, hh, :].astype(BF16)
            s = lax.dot_general(qb[:, sl].astype(BF16), kh, _NT, preferred_element_type=F32) * scale
            p = jnp.exp(s - jnp.max(s, axis=-1, keepdims=True))
            p = p / jnp.sum(p, axis=-1, keepdims=True)
            outs.append(_dot(p.astype(BF16), vh))
        o = jnp.concatenate(outs, axis=-1)
        o_ref[bi * rows:(bi + 1) * rows, :] = o[0:rows].astype(o_ref.dtype)


def _mem_attention(q, mem_k, mem_v, *, kv0, row0, nseq, tb, nb, name):
    d = q.shape[1]
    _, mlen, heads, hd = mem_k.shape
    assert kv0 % nb == 0
    kvb = kv0 // nb
    if nb > 1:
        rows, nt = tb, 1
    else:
        rows = _pick_tile(tb, 512)
        nt = tb // rows
    assert nseq % nb == 0 and row0 % (nb * rows) == 0
    rb0 = row0 // (nb * rows)
    return pl.pallas_call(
        functools.partial(_attn_kernel, nb=nb, rows=rows, heads=heads, hd=hd),
        grid=(nseq // nb, nt),
        in_specs=[pl.BlockSpec((nb * rows, d), lambda i, t: (rb0 + i * nt + t, 0)),
                  pl.BlockSpec((nb, mlen, heads, hd), lambda i, t: (kvb + i, 0, 0, 0)),
                  pl.BlockSpec((nb, mlen, heads, hd), lambda i, t: (kvb + i, 0, 0, 0))],
        out_specs=pl.BlockSpec((nb * rows, d), lambda i, t: (i * nt + t, 0)),
        out_shape=jax.ShapeDtypeStruct((nseq * tb, d), BF16),
        compiler_params=_params("parallel", "parallel"),
        name=name,
    )(q, mem_k, mem_v)


def _tile_copy(src_hbm, tok, dst, r, sem, nsub):
    so = tok * nsub if isinstance(tok, int) else pl.multiple_of(tok * nsub, nsub)
    do = pl.multiple_of(r * nsub, nsub)
    return pltpu.make_async_copy(src_hbm.at[pl.ds(so, nsub)], dst.at[pl.ds(do, nsub)], sem)


def _gather_start(src_hbm, idx_ref, n, dst, sem, nsub):
    def body(r, c):
        _tile_copy(src_hbm, idx_ref[r], dst, r, sem, nsub).start()
        return c

    lax.fori_loop(0, n, body, 0, unroll=8)


def _gather_wait(src_hbm, n, dst, sem, nsub):
    def body(r, c):
        _tile_copy(src_hbm, 0, dst, r, sem, nsub).wait()
        return c

    lax.fori_loop(0, n, body, 0, unroll=8)


def _expert_kernel(te_ref, nu_ref, idx0_ref, idx1_ref, x_hbm, wg_ref, wu_ref, wd_ref, o_ref,
                   xbuf, wgb, wub, wdb, sem, *, d):
    i = pl.program_id(0)
    nu = nu_ref[0]
    nsub = d // LANES

    @pl.when(i == 0)
    def _():
        _gather_start(x_hbm, idx0_ref, EXPERT_TILE, xbuf.at[0], sem.at[0], nsub)

    @pl.when(i + 1 < nu)
    def _():
        nxt = (i + 1) % 2
        _gather_start(x_hbm, idx1_ref, EXPERT_TILE, xbuf.at[nxt], sem.at[nxt], nsub)

    @pl.when(i < nu)
    def _():
        @pl.when((i == 0) | (te_ref[i] != te_ref[jnp.maximum(i - 1, 0)]))
        def _():
            wgb[...] = wg_ref[0].astype(BF16)
            wub[...] = wu_ref[0].astype(BF16)
            wdb[...] = wd_ref[0].astype(BF16)

        slot = i % 2
        _gather_wait(x_hbm, EXPERT_TILE, xbuf.at[slot], sem.at[slot], nsub)
        x = _load_token_tiles(xbuf.at[slot], EXPERT_TILE, d).astype(BF16)
        gate = _dot(x, wgb[...])
        up = _dot(x, wub[...])
        hid = (gate * _sigmoid(gate) * up).astype(BF16)
        _store_token_tiles(o_ref, _dot(hid, wdb[...]))

    @pl.when(i >= nu)
    def _():
        o_ref[...] = jnp.zeros(o_ref.shape, F32)


def _experts(xt, src_tok, tile_expert, n_used, wg, wu, wd):
    d, f = wg.shape[1], wg.shape[2]
    nsub = d // LANES
    p = src_tok.shape[0]
    nt = p // EXPERT_TILE
    live = lambda i, te, nu: jnp.minimum(i, nu[0] - 1)
    wspec = lambda shp: pl.BlockSpec(shp, lambda i, te, nu: (te[live(i, te, nu)], 0, 0))
    grid_spec = pltpu.PrefetchScalarGridSpec(
        num_scalar_prefetch=2,
        grid=(nt,),
        in_specs=[pl.BlockSpec((EXPERT_TILE,), lambda i, te, nu: (i,), memory_space=pltpu.SMEM),
                  pl.BlockSpec((EXPERT_TILE,), lambda i, te, nu: (jnp.minimum(i + 1, nt - 1),), memory_space=pltpu.SMEM),
                  pl.BlockSpec(memory_space=pl.ANY),
                  wspec((1, d, f)), wspec((1, d, f)), wspec((1, f, d))],
        out_specs=pl.BlockSpec((EXPERT_TILE * nsub, LANES), lambda i, te, nu: (i, 0)),
        scratch_shapes=[pltpu.VMEM((2, EXPERT_TILE * nsub, LANES), F32),
                        pltpu.VMEM((d, f), BF16), pltpu.VMEM((d, f), BF16), pltpu.VMEM((f, d), BF16),
                        pltpu.SemaphoreType.DMA((2,))],
    )
    return pl.pallas_call(
        functools.partial(_expert_kernel, d=d),
        grid_spec=grid_spec,
        out_shape=jax.ShapeDtypeStruct((p * nsub, LANES), F32),
        compiler_params=_params("arbitrary"),
        name="moe_experts",
    )(tile_expert, n_used, src_tok, src_tok, xt, wg, wu, wd)


def _combine_ln_kernel(ia0_ref, ia1_ref, ib0_ref, ib1_ref, y_hbm, r_ref, x_ref, g_ref, b_ref, xf_ref, xb_ref,
                       ybuf, sem, *, alpha, nsteps):
    i = pl.program_id(0)
    rows, d = x_ref.shape
    nsub = d // LANES

    def start(idx_a, idx_b, slot):
        _gather_start(y_hbm, idx_a, rows, ybuf.at[slot, 0], sem.at[slot, 0], nsub)
        _gather_start(y_hbm, idx_b, rows, ybuf.at[slot, 1], sem.at[slot, 1], nsub)

    @pl.when(i == 0)
    def _():
        start(ia0_ref, ib0_ref, 0)

    @pl.when(i + 1 < nsteps)
    def _():
        start(ia1_ref, ib1_ref, (i + 1) % 2)

    slot = i % 2
    _gather_wait(y_hbm, rows, ybuf.at[slot, 0], sem.at[slot, 0], nsub)
    _gather_wait(y_hbm, rows, ybuf.at[slot, 1], sem.at[slot, 1], nsub)
    w0 = r_ref[:, 0:1]
    w1 = r_ref[:, 1:2]
    moe = w0 * _load_token_tiles(ybuf.at[slot, 0], rows, d) + w1 * _load_token_tiles(ybuf.at[slot, 1], rows, d)
    y = _layer_norm_rows(alpha * x_ref[...] + moe, g_ref[...], b_ref[...])
    xf_ref[...] = y
    xb_ref[...] = y.astype(BF16)


def _combine_ln(ys, dest_a, dest_b, rinfo, x, g, b, *, alpha, name):
    n, d = x.shape
    nsub = d // LANES
    tm = _pick_tile(n, 512)
    nsteps = n // tm
    row = lambda i: (i, 0)
    fix = lambda i: (0, 0)
    cur = pl.BlockSpec((tm,), lambda i: (i,), memory_space=pltpu.SMEM)
    nxt = pl.BlockSpec((tm,), lambda i: (jnp.minimum(i + 1, nsteps - 1),), memory_space=pltpu.SMEM)
    return pl.pallas_call(
        functools.partial(_combine_ln_kernel, alpha=alpha, nsteps=nsteps),
        grid=(nsteps,),
        in_specs=[cur, nxt, cur, nxt, pl.BlockSpec(memory_space=pl.ANY),
                  pl.BlockSpec((tm, LANES), row), pl.BlockSpec((tm, d), row),
                  pl.BlockSpec((1, d), fix), pl.BlockSpec((1, d), fix)],
        out_specs=[pl.BlockSpec((tm, d), row), pl.BlockSpec((tm, d), row)],
        out_shape=[jax.ShapeDtypeStruct((n, d), F32), jax.ShapeDtypeStruct((n, d), BF16)],
        scratch_shapes=[pltpu.VMEM((2, 2, tm * nsub, LANES), F32), pltpu.SemaphoreType.DMA((2, 2))],
        compiler_params=_params("arbitrary"),
        name=name,
    )(dest_a, dest_a, dest_b, dest_b, ys, rinfo, x, g.reshape(1, d), b.reshape(1, d))


def _dispatch_plan(rinfo, n_exp_total):
    n = rinfo.shape[0]
    a = n * TOP_K
    eid = rinfo[:, 2:2 + TOP_K].astype(jnp.int32).reshape(a)
    onehot = (eid[:, None] == jnp.arange(n_exp_total, dtype=jnp.int32)[None, :]).astype(jnp.int32)
    csum = jnp.cumsum(onehot, axis=0)
    rank = jnp.sum(csum * onehot, axis=1) - 1
    counts = csum[-1]
    padded = ((counts + EXPERT_TILE - 1) // EXPERT_TILE) * EXPERT_TILE
    ends = jnp.cumsum(padded)
    starts = ends - padded
    dest = starts[eid] + rank
    nt = -(-a // EXPERT_TILE) + n_exp_total
    p = nt * EXPERT_TILE
    src_tok = jnp.zeros((p,), jnp.int32).at[dest].set(jnp.arange(a, dtype=jnp.int32) // TOP_K)
    tile_start = jnp.arange(nt, dtype=jnp.int32) * EXPERT_TILE
    tile_expert = jnp.minimum(jnp.sum(tile_start[:, None] >= ends[None, :], axis=1), n_exp_total - 1).astype(jnp.int32)
    n_used = (ends[-1] // EXPERT_TILE).astype(jnp.int32).reshape(1)
    return dest.astype(jnp.int32), src_tok, tile_expert, n_used


def kernel(x_prompt, x_sample, state_hgrn, state_conv, state_gla, cache_mem_k, cache_mem_v, mem_prompt, w_in_a, lb_a, conv_w, gnorm_a, w_out_a, w_in_c, w_gk2, b_gk2, gnorm_c, w_out_c, w_mq, w_mk, w_mv, w_mo, ln_g, ln_b, w_grp, b_grp, w_er, b_er, w_gate, w_up, w_down):
    bsz, seq, d = x_prompt.shape
    dec_b, dec_t, _ = x_sample.shape
    depth = w_mq.shape[0]
    _, _, ha_h, ha_dk, ha_dv = state_hgrn.shape
    _, _, gl_h, gl_dk, gl_dv = state_gla.shape
    _, _, mlen, mem_h, mem_hd = cache_mem_k.shape
    n_groups, n_exp = w_er.shape[1], w_er.shape[3]
    sc_w = conv_w.shape[1]
    sc_k = conv_w.shape[2]
    ha_f, ha_w = ha_h * ha_dk, ha_h * ha_dv
    gl_key, gl_val = gl_h * gl_dk, gl_h * gl_dv
    rank = w_gk2.shape[1]
    alpha = (2.0 * depth) ** 0.25
    assert dec_t <= SAMPLE_ROWS and sc_k == 3 and seq % CHUNK == 0 and rank <= LANES
    assert n_groups + n_groups * n_exp <= LANES

    rows_p = bsz * seq
    rows_s = dec_b * SAMPLE_ROWS
    xs_pad = jnp.pad(x_sample, ((0, 0), (0, SAMPLE_ROWS - dec_t), (0, 0)))
    x = jnp.concatenate([x_prompt.reshape(rows_p, d), xs_pad.reshape(rows_s, d)], axis=0)
    xb = x.astype(BF16)
    consts = _rec_constants()

    mem_b = mem_prompt.astype(BF16)
    mk_p = _mem_kv(mem_b, w_mk.astype(BF16), mem_h, "mem_k_proj")
    mv_p = _mem_kv(mem_b, w_mv.astype(BF16), mem_h, "mem_v_proj")
    ck = cache_mem_k.reshape(depth * dec_b, mlen, mem_h, mem_hd)
    cv = cache_mem_v.reshape(depth * dec_b, mlen, mem_h, mem_hd)
    st_hgrn = state_hgrn.reshape(-1, ha_h, ha_dk, ha_dv)
    st_gla = state_gla.reshape(-1, gl_h, gl_dk, gl_dv)

    lb = jnp.cumsum(jax.nn.softmax(lb_a.astype(F32), axis=0), axis=0)
    lb = lb - lb[0]
    log_lb, log1m_lb, onem_lb = jnp.log(lb), jnp.log1p(-lb), 1.0 - lb

    z_hgrn = jnp.zeros((bsz, ha_h, ha_dk, ha_dv), F32)
    z_conv = jnp.zeros((bsz, sc_k - 1, sc_w), F32)
    z_gla = jnp.zeros((bsz, gl_h, gl_dk, gl_dv), F32)

    hgrn_p, conv_p, gla_p, hgrn_s, conv_s, gla_s = [], [], [], [], [], []
    for l in range(depth):
        j = l // 2
        if l % 2 == 0:
            y = _matmul(xb, w_in_a[j].astype(BF16), tn=512, name=f"in_proj_{l}")
            extras = (log_lb[j:j + 1], log1m_lb[j:j + 1], onem_lb[j:j + 1])
            cols = dict(mode="hgrn", col_q=0, col_k=ha_f, col_v=2 * ha_f, col_g=2 * ha_f + ha_w)
            o_p, s_p = _recurrence(y, extras, gnorm_a[j], z_hgrn, consts, row0=0, nseq=bsz, tb=seq,
                                   t_valid=seq, hb=1, name=f"hgrn_prompt_{l}", **cols)
            o_s, s_s = _recurrence(y, extras, gnorm_a[j], st_hgrn, consts, row0=rows_p, nseq=dec_b, s_off=j * dec_b,
                                   tb=SAMPLE_ROWS, t_valid=dec_t, hb=ha_h, name=f"hgrn_sample_{l}", **cols)
            c0 = 2 * ha_f + 2 * ha_w
            w3 = conv_w[j].T
            ccols = dict(col_b=c0, col_c=c0 + sc_w, col_h=c0 + 2 * sc_w, width=sc_w)
            c_p, t_p = _short_conv(y, w3, z_conv, row0=0, nseq=bsz, tb=seq, name=f"conv_prompt_{l}", **ccols)
            c_s, t_s = _short_conv(y, w3, state_conv[j], row0=rows_p, nseq=dec_b, tb=SAMPLE_ROWS,
                                   name=f"conv_sample_{l}", **ccols)
            hgrn_p.append(s_p)
            hgrn_s.append(s_s)
            conv_p.append(t_p[:, SUBLANES - (sc_k - 1):])
            conv_s.append(t_s[:, dec_t - (sc_k - 1):dec_t])
            acts = [(o_p, o_s), (c_p, c_s)]
            wo = w_out_a[j].astype(BF16)
            weights = [wo[:ha_w], wo[ha_w:]]
        else:
            ncol = 2 * gl_key + 2 * gl_val
            w_in = jnp.pad(w_in_c[j], ((0, 0), (0, LANES - rank))).astype(BF16)
            y = _matmul(xb, w_in, tn=640 if (ncol + LANES) % 640 == 0 else LANES, name=f"in_proj_{l}")
            wg2 = jnp.pad(w_gk2[j], ((0, LANES - rank), (0, 0)))
            wgh = wg2.astype(BF16)
            wgl = (wg2 - wgh.astype(F32)).astype(BF16)
            extras = (wgh, wgl, b_gk2[j].reshape(1, gl_key))
            cols = dict(mode="gla", col_q=0, col_k=gl_key, col_v=2 * gl_key, col_g=2 * gl_key + gl_val, col_gz=ncol)
            o_p, s_p = _recurrence(y, extras, gnorm_c[j], z_gla, consts, row0=0, nseq=bsz, tb=seq,
                                   t_valid=seq, hb=1, name=f"gla_prompt_{l}", **cols)
            o_s, s_s = _recurrence(y, extras, gnorm_c[j], st_gla, consts, row0=rows_p, nseq=dec_b, s_off=j * dec_b,
                                   tb=SAMPLE_ROWS, t_valid=dec_t, hb=gl_h, name=f"gla_sample_{l}", **cols)
            gla_p.append(s_p)
            gla_s.append(s_s)
            acts = [(o_p, o_s)]
            weights = [w_out_c[j].astype(BF16)]
        x, xb = _proj_ln(acts, weights, x, ln_g[l, 0], ln_b[l, 0], alpha=alpha, name=f"mix_out_ln_{l}")

        q = _matmul(xb, w_mq[l].astype(BF16), tn=512, name=f"mem_q_{l}")
        a_p = _mem_attention(q, mk_p, mv_p, kv0=l * bsz, row0=0, nseq=bsz, tb=seq, nb=1, name=f"mem_attn_prompt_{l}")
        a_s = _mem_attention(q, ck, cv, kv0=l * dec_b, row0=rows_p, nseq=dec_b, tb=SAMPLE_ROWS,
                             nb=4 if dec_b % 4 == 0 else 1, name=f"mem_attn_sample_{l}")
        wr = jnp.concatenate([w_grp[l]] + [w_er[l, gi] for gi in range(n_groups)], axis=1)
        wr = jnp.pad(wr, ((0, 0), (0, LANES - wr.shape[1])))
        wrh = wr.astype(BF16)
        wrl = (wr - wrh.astype(F32)).astype(BF16)
        br = jnp.concatenate([b_grp[l], b_er[l].reshape(-1)])
        br = jnp.pad(br, (0, LANES - br.shape[0])).reshape(1, LANES)
        x, xb, rinfo, xt = _proj_ln([(a_p, a_s)], [w_mo[l].astype(BF16)], x, ln_g[l, 1],
                                    ln_b[l, 1], alpha=alpha, router_w=(wrh, wrl, br), router=(n_groups, n_exp),
                                    name=f"mem_out_ln_router_{l}")

        ne = n_groups * n_exp
        nsub = d // LANES
        dest, src_tok, tile_expert, n_used = _dispatch_plan(rinfo, ne)
        f = w_gate.shape[-1]
        ys = _experts(xt, src_tok, tile_expert + l * ne, n_used, w_gate.reshape(depth * ne, d, f),
                      w_up.reshape(depth * ne, d, f), w_down.reshape(depth * ne, f, d))
        x, xb = _combine_ln(ys, dest[0::TOP_K], dest[1::TOP_K], rinfo, x, ln_g[l, 2], ln_b[l, 2],
                            alpha=alpha, name=f"moe_combine_ln_{l}")

    y_prompt = x[:rows_p].reshape(bsz, seq, d)
    y_sample = x[rows_p:].reshape(dec_b, SAMPLE_ROWS, d)[:, :dec_t]
    return (y_prompt, y_sample, jnp.stack(hgrn_p), jnp.stack(conv_p), jnp.stack(gla_p),
            mk_p.reshape(depth, bsz, mlen, mem_h, mem_hd), mv_p.reshape(depth, bsz, mlen, mem_h, mem_hd),
            jnp.stack(hgrn_s), jnp.stack(conv_s), jnp.stack(gla_s))
```

```python
import functools
import math

import numpy as np
import jax
import jax.numpy as jnp
from jax import lax
from jax.experimental import pallas as pl
from jax.experimental.pallas import tpu as pltpu

F32 = jnp.float32
BF16 = jnp.bfloat16

LN_EPS = 1e-5
RMS_EPS = 1e-6
GLA_NORMALIZER = 16.0
TOP_K = 2

LANES = 128
SUBLANES = 8
CHUNK = 128
SAMPLE_ROWS = 8
EXPERT_TILE = 256
VMEM_LIMIT = 48 * 1024 * 1024

_NT = (((1,), (1,)), ((), ()))
_TN = (((0,), (0,)), ((), ()))


def _params(*sem):
    return pltpu.CompilerParams(dimension_semantics=sem, vmem_limit_bytes=VMEM_LIMIT)


def _pick_tile(n, pref):
    t = pref
    while t > SUBLANES and n % t:
        t //= 2
    assert n % t == 0, (n, pref)
    return t


def _dot(a, b):
    return jnp.dot(a, b, preferred_element_type=F32)


def _sigmoid(x):
    return 1.0 / (1.0 + jnp.exp(-x))


def _log_sigmoid(x):
    return jnp.minimum(x, 0.0) - jnp.log1p(jnp.exp(-jnp.abs(x)))


def _split2(a):
    hi = a.astype(BF16)
    lo = (a - hi.astype(F32)).astype(BF16)
    return hi, lo


def _split3(a):
    hi = a.astype(BF16)
    r = a - hi.astype(F32)
    mid = r.astype(BF16)
    lo = (r - mid.astype(F32)).astype(BF16)
    return hi, mid, lo


def _pad_rows(x, n):
    if x.shape[0] == n:
        return x
    return jnp.concatenate([x, jnp.zeros((n - x.shape[0], x.shape[1]), x.dtype)], axis=0)


def _store_token_tiles(ref, y):
    rows, d = y.shape
    nsub = d // LANES
    for s in range(nsub):
        ref[pl.ds(s, rows, stride=nsub), :] = y[:, s * LANES:(s + 1) * LANES]


def _load_token_tiles(ref, rows, d):
    nsub = d // LANES
    return jnp.concatenate([ref[pl.ds(s, rows, stride=nsub), :] for s in range(nsub)], axis=1)


def _mm_kernel(x_ref, w_ref, o_ref):
    o_ref[...] = _dot(x_ref[...], w_ref[...]).astype(o_ref.dtype)


def _col_tile(m, pref=1024):
    return max(t for t in range(LANES, pref + 1, LANES) if m % t == 0)


def _matmul(x, w, *, out_dtype=F32, name):
    n, k = x.shape
    m = w.shape[1]
    tm = _pick_tile(n, 1024)
    tn = _col_tile(m)
    return pl.pallas_call(
        _mm_kernel,
        grid=(n // tm, m // tn),
        in_specs=[pl.BlockSpec((tm, k), lambda i, j: (i, 0)),
                  pl.BlockSpec((k, tn), lambda i, j: (0, j))],
        out_specs=pl.BlockSpec((tm, tn), lambda i, j: (i, j)),
        out_shape=jax.ShapeDtypeStruct((n, m), out_dtype),
        compiler_params=_params("parallel", "parallel"),
        name=name,
    )(x, w)


def _mem_kv_kernel(x_ref, w_ref, o_ref, *, heads):
    y = _dot(x_ref[0], w_ref[0])
    hd = y.shape[1] // heads
    for hh in range(heads):
        o_ref[0, :, hh, :] = y[:, hh * hd:(hh + 1) * hd]


def _mem_kv(mem_bf16, w_all, heads, name):
    nb, mlen, d = mem_bf16.shape
    nj, _, dout = w_all.shape
    return pl.pallas_call(
        functools.partial(_mem_kv_kernel, heads=heads),
        grid=(nj, nb),
        in_specs=[pl.BlockSpec((1, mlen, d), lambda j, i: (i, 0, 0)),
                  pl.BlockSpec((1, d, dout), lambda j, i: (j, 0, 0))],
        out_specs=pl.BlockSpec((1, mlen, heads, dout // heads), lambda j, i: (j * nb + i, 0, 0, 0)),
        out_shape=jax.ShapeDtypeStruct((nj * nb, mlen, heads, dout // heads), F32),
        compiler_params=_params("parallel", "parallel"),
        name=name,
    )(mem_bf16, w_all)


def _layer_norm_rows(z, g, b):
    mu = jnp.mean(z, axis=-1, keepdims=True)
    zc = z - mu
    var = jnp.mean(zc * zc, axis=-1, keepdims=True)
    return zc * lax.rsqrt(var + LN_EPS) * g + b


def _route(logits, n_groups, n_exp):
    lane = lax.broadcasted_iota(jnp.int32, logits.shape, 1).astype(F32)
    big = jnp.float32(1 << 20)
    neg = jnp.float32(-jnp.inf)
    gl = jnp.where(lane < n_groups, logits, neg)
    gmax = jnp.max(gl, axis=-1, keepdims=True)
    gsel = jnp.min(jnp.where(gl == gmax, lane, big), axis=-1, keepdims=True)
    gden = jnp.sum(jnp.exp(gl - gmax), axis=-1, keepdims=True)
    g_w = 1.0 / gden
    lo = n_groups + gsel * n_exp
    el = jnp.where((lane >= lo) & (lane < lo + n_exp), logits, neg)
    m1 = jnp.max(el, axis=-1, keepdims=True)
    i1 = jnp.min(jnp.where(el == m1, lane, big), axis=-1, keepdims=True)
    el2 = jnp.where(lane == i1, neg, el)
    m2 = jnp.max(el2, axis=-1, keepdims=True)
    i2 = jnp.min(jnp.where(el2 == m2, lane, big), axis=-1, keepdims=True)
    t = jnp.exp(m2 - m1)
    w1 = g_w / (1.0 + t)
    w2 = g_w * t / (1.0 + t)
    e1 = i1 - n_groups
    e2 = i2 - n_groups
    zero = jnp.zeros(logits.shape, F32)
    return jnp.where(lane == 0, w1, jnp.where(lane == 1, w2, jnp.where(lane == 2, e1, jnp.where(lane == 3, e2, zero))))


def _proj_ln_kernel(*refs, n_in, alpha, router, prompt_tiles):
    ap_refs = refs[:n_in]
    as_refs = refs[n_in:2 * n_in]
    w_refs = refs[2 * n_in:3 * n_in]
    x_ref, g_ref, b_ref = refs[3 * n_in:3 * n_in + 3]
    pos = 3 * n_in + 3
    if router is not None:
        wrh_ref, wrl_ref, br_ref = refs[pos:pos + 3]
        pos += 3
    xf_ref, xb_ref = refs[pos:pos + 2]
    is_prompt = pl.program_id(0) < prompt_tiles
    acc = None
    for ap_ref, as_ref, w_ref in zip(ap_refs, as_refs, w_refs):
        part = _dot(jnp.where(is_prompt, ap_ref[...], as_ref[...]), w_ref[...])
        acc = part if acc is None else acc + part
    y = _layer_norm_rows(alpha * x_ref[...] + acc, g_ref[...], b_ref[...])
    xf_ref[...] = y
    xb_ref[...] = y.astype(BF16)
    if router is not None:
        hi, lo = _split2(y)
        logits = _dot(hi, wrh_ref[...]) + _dot(lo, wrh_ref[...]) + _dot(hi, wrl_ref[...]) + br_ref[...]
        refs[pos + 2][...] = _route(logits, *router)
        _store_token_tiles(refs[pos + 3], y)


def _proj_ln(acts, weights, x, g, b, *, alpha, router_w=None, router=None, name):
    n, d = x.shape
    n_p = acts[0][0].shape[0]
    tm = _pick_tile(math.gcd(n_p, n - n_p), 512)
    npt = n_p // tm
    row = lambda i: (i, 0)
    fix = lambda i: (0, 0)
    in_specs = [pl.BlockSpec((tm, a[0].shape[1]), lambda i: (jnp.minimum(i, npt - 1), 0)) for a in acts]
    in_specs += [pl.BlockSpec((tm, a[1].shape[1]), lambda i: (jnp.maximum(i - npt, 0), 0)) for a in acts]
    in_specs += [pl.BlockSpec(w.shape, fix) for w in weights]
    in_specs += [pl.BlockSpec((tm, d), row), pl.BlockSpec((1, d), fix), pl.BlockSpec((1, d), fix)]
    args = [a[0] for a in acts] + [a[1] for a in acts] + list(weights) + [x, g.reshape(1, d), b.reshape(1, d)]
    out_specs = [pl.BlockSpec((tm, d), row), pl.BlockSpec((tm, d), row)]
    out_shape = [jax.ShapeDtypeStruct((n, d), F32), jax.ShapeDtypeStruct((n, d), BF16)]
    if router_w is not None:
        in_specs += [pl.BlockSpec(r.shape, fix) for r in router_w]
        args += list(router_w)
        out_specs.append(pl.BlockSpec((tm, LANES), row))
        out_shape.append(jax.ShapeDtypeStruct((n, LANES), F32))
        out_specs.append(pl.BlockSpec((tm * (d // LANES), LANES), row))
        out_shape.append(jax.ShapeDtypeStruct((n * (d // LANES), LANES), F32))
    return pl.pallas_call(
        functools.partial(_proj_ln_kernel, n_in=len(acts), alpha=alpha, router=router, prompt_tiles=npt),
        grid=(n // tm,),
        in_specs=in_specs, out_specs=out_specs, out_shape=out_shape,
        compiler_params=_params("parallel"),
        name=name,
    )(*args)


_LEVELS = (8, 16, 32, 64)


def _rec_constants():
    t = np.arange(CHUNK)
    tri = (t[:, None] >= t[None, :]).astype(np.float32)
    masks = []
    for lvl in _LEVELS:
        bt, bs = t[:, None] // lvl, t[None, :] // lvl
        masks.append(((bt - bs == 1) & (bt % 2 == 1)).astype(np.float32))
    return jnp.asarray(tri, BF16), jnp.asarray(np.stack(masks), F32)


def _rec_chunk(q, k, a, v, st, tri_ref, mask_ref, nv):
    r = SUBLANES * nv
    m = max(r, 2 * SUBLANES)
    dk = q.shape[1]

    hi, mid, lo = _split3(_pad_rows(a, CHUNK))
    tri = tri_ref[0:m, :]
    b = (_dot(tri, hi) + _dot(tri, mid) + _dot(tri, lo))[0:r]
    b_last = b[r - 1:r]

    scores = None
    for li, lvl in enumerate(_LEVELS):
        if lvl >= r:
            continue
        qp, kp = [], []
        for blk in range(r // lvl):
            rows = slice(blk * lvl, (blk + 1) * lvl)
            if blk % 2 == 1:
                rho = b[blk * lvl - 1:blk * lvl]
                qp.append(q[rows] * jnp.exp(b[rows] - rho))
                kp.append(jnp.zeros((lvl, dk), F32))
            else:
                rho = b[(blk + 1) * lvl - 1:(blk + 1) * lvl]
                kp.append(k[rows] * jnp.exp(rho - b[rows]))
                qp.append(jnp.zeros((lvl, dk), F32))
        qh = jnp.concatenate(qp, axis=0).astype(BF16)
        kh = jnp.concatenate(kp, axis=0).astype(BF16)
        s = lax.dot_general(qh, kh, _NT, preferred_element_type=F32) * mask_ref[li]
        scores = s if scores is None else scores + s

    lane = lax.broadcasted_iota(jnp.int32, (SUBLANES, CHUNK), 1)
    sub = lax.broadcasted_iota(jnp.int32, (SUBLANES, CHUNK), 0)
    dparts = []
    for j in range(nv):
        rows = slice(SUBLANES * j, SUBLANES * (j + 1))
        qj, kj, bj = q[rows], k[rows], b[rows]
        acc = jnp.zeros((SUBLANES, CHUNK), F32)
        for s_ in range(SUBLANES):
            e = jnp.exp(jnp.minimum(bj - bj[s_:s_ + 1], 0.0))
            col = jnp.sum(qj * kj[s_:s_ + 1] * e, axis=1, keepdims=True)
            acc = jnp.where(lane == SUBLANES * j + s_, col, acc)
        dparts.append(jnp.where(lane <= sub + SUBLANES * j, acc, 0.0))
    diag = _pad_rows(jnp.concatenate(dparts, axis=0) if nv > 1 else dparts[0], m)
    scores = diag if scores is None else scores + diag

    v_bf = _pad_rows(v, CHUNK).astype(BF16)
    o = _dot(scores.astype(BF16), v_bf)
    qe = _pad_rows(q * jnp.exp(b), m).astype(BF16)
    o = o + lax.dot_general(qe, st.astype(BF16), _NT, preferred_element_type=F32)
    kd = _pad_rows(k * jnp.exp(b_last - b), CHUNK).astype(BF16)
    st_new = st * jnp.exp(b_last) + lax.dot_general(v_bf, kd, _TN, preferred_element_type=F32)
    return o[0:r], st_new


def _rec_kernel(*refs, mode, hb, dk, dv, nv, nchunks, t_valid, qscale):
    q_ref, k_ref, v_ref, g_ref = refs[:4]
    pos = 4
    if mode == "gla":
        gz_ref, wgh_ref, wgl_ref, bg_ref = refs[pos:pos + 4]
        pos += 4
    else:
        loglb_ref, log1m_ref, onem_ref = refs[pos:pos + 3]
        pos += 3
    nw_ref, s0_ref, tri_ref, mask_ref = refs[pos:pos + 4]
    o_ref, so_ref, st_scr = refs[pos + 4:pos + 7]
    r = SUBLANES * nv
    tstep = pl.program_id(2)

    @pl.when(tstep == 0)
    def _():
        for hh in range(hb):
            st_scr[hh] = s0_ref[0, hh].T

    def chunk_step(c, carry):
        rsl = pl.ds(c * r if isinstance(c, int) else pl.multiple_of(c * r, r), r)
        if t_valid < r:
            valid_k = lax.broadcasted_iota(jnp.int32, (r, dk), 0) < t_valid
            valid_v = lax.broadcasted_iota(jnp.int32, (r, dv), 0) < t_valid
        if mode == "gla":
            gz = _pad_rows(gz_ref[rsl, :], max(r, 2 * SUBLANES))
            gzh, gzl = _split2(gz)
        for hh in range(hb):
            ksl = slice(hh * dk, (hh + 1) * dk)
            vsl = slice(hh * dv, (hh + 1) * dv)
            qr = q_ref[rsl, ksl]
            kr = k_ref[rsl, ksl]
            v = v_ref[rsl, vsl]
            if mode == "gla":
                gk = (_dot(gzh, wgh_ref[:, ksl]) + _dot(gzl, wgh_ref[:, ksl]) + _dot(gzh, wgl_ref[:, ksl]))[0:r]
                a = _log_sigmoid(gk + bg_ref[:, ksl]) / GLA_NORMALIZER
                q = qr * qscale
                k = kr
            else:
                ls = _log_sigmoid(kr)
                x2 = log1m_ref[:, ksl] + ls
                x1 = loglb_ref[:, ksl]
                a = jnp.maximum(x1, x2) + jnp.log1p(jnp.exp(-jnp.abs(x1 - x2)))
                k = onem_ref[:, ksl] * _sigmoid(-kr)
                q = qr * _sigmoid(qr)
            if t_valid < r:
                a = jnp.where(valid_k, a, 0.0)
                k = jnp.where(valid_k, k, 0.0)
                v = jnp.where(valid_v, v, 0.0)
            o, st_new = _rec_chunk(q, k, a, v, st_scr[hh], tri_ref, mask_ref, nv)
            st_scr[hh] = st_new
            g = g_ref[rsl, vsl]
            y = o * lax.rsqrt(jnp.mean(o * o, axis=-1, keepdims=True) + RMS_EPS) * nw_ref[...]
            o_ref[rsl, vsl] = (y * (g * _sigmoid(g))).astype(o_ref.dtype)
        return carry

    if nchunks == 1:
        chunk_step(0, 0)
    else:
        lax.fori_loop(0, nchunks, chunk_step, 0)

    @pl.when(tstep == pl.num_programs(2) - 1)
    def _():
        for hh in range(hb):
            so_ref[0, hh] = st_scr[hh].T


def _recurrence(y, extras, norm_w, s0, consts, *, mode, col_q, col_k, col_v, col_g, col_gz=None,
                row0, nseq, tb, t_valid, hb, name, s_off=0, tblk=None):
    _, h, dk, dv = s0.shape
    tblk = tb if tblk is None else tblk
    assert h % hb == 0 and row0 % tblk == 0 and tb % tblk == 0
    nt = tb // tblk
    nv = min(tblk, CHUNK) // SUBLANES
    nchunks = tblk // (SUBLANES * nv)
    rb0 = row0 // tblk
    kw, vw = hb * dk, hb * dv
    tri, masks = consts

    def cspec(width, col):
        assert col % width == 0
        cb = col // width
        return pl.BlockSpec((tblk, width), lambda i, j, t: (rb0 + i * nt + t, cb + j))

    fixh = lambda width: pl.BlockSpec((1, width), lambda i, j, t: (0, j))
    in_specs = [cspec(kw, col_q), cspec(kw, col_k), cspec(vw, col_v), cspec(vw, col_g)]
    args = [y, y, y, y]
    if mode == "gla":
        wgh, wgl, bg = extras
        in_specs += [pl.BlockSpec((tblk, LANES), lambda i, j, t: (rb0 + i * nt + t, col_gz // LANES)),
                     pl.BlockSpec((LANES, kw), lambda i, j, t: (0, j)),
                     pl.BlockSpec((LANES, kw), lambda i, j, t: (0, j)), fixh(kw)]
        args += [y, wgh, wgl, bg]
    else:
        in_specs += [fixh(kw), fixh(kw), fixh(kw)]
        args += list(extras)
    in_specs += [pl.BlockSpec((1, dv), lambda i, j, t: (0, 0)),
                 pl.BlockSpec((1, hb, dk, dv), lambda i, j, t: (s_off + i, j, 0, 0)),
                 pl.BlockSpec(tri.shape, lambda i, j, t: (0, 0)),
                 pl.BlockSpec(masks.shape, lambda i, j, t: (0, 0, 0))]
    args += [norm_w.reshape(1, dv), s0, tri, masks]
    return pl.pallas_call(
        functools.partial(_rec_kernel, mode=mode, hb=hb, dk=dk, dv=dv, nv=nv, nchunks=nchunks,
                          t_valid=t_valid, qscale=dk ** -0.5),
        grid=(nseq, h // hb, nt),
        in_specs=in_specs,
        out_specs=[pl.BlockSpec((tblk, vw), lambda i, j, t: (i * nt + t, j)),
                   pl.BlockSpec((1, hb, dk, dv), lambda i, j, t: (i, j, 0, 0))],
        out_shape=[jax.ShapeDtypeStruct((nseq * tb, h * dv), BF16),
                   jax.ShapeDtypeStruct((nseq, h, dk, dv), F32)],
        scratch_shapes=[pltpu.VMEM((hb, dv, dk), F32)],
        compiler_params=_params("parallel", "parallel", "arbitrary"),
        name=name,
    )(*args)


def _conv_kernel(bg_ref, cg_ref, h_ref, w_ref, buf_ref, y_ref, tail_ref, carry):
    t = pl.program_id(1)
    u = cg_ref[...] * h_ref[...]
    tt = u.shape[0]

    @pl.when(t == 0)
    def _():
        carry[...] = jnp.zeros(carry.shape, F32)
        carry[SUBLANES - 2:SUBLANES, :] = buf_ref[0]

    p6 = carry[SUBLANES - 2:SUBLANES - 1, :]
    p7 = carry[SUBLANES - 1:SUBLANES, :]
    row = lax.broadcasted_iota(jnp.int32, u.shape, 0)
    u1 = jnp.where(row == 0, p7, pltpu.roll(u, 1, axis=0))
    u2 = jnp.where(row == 0, p6, jnp.where(row == 1, p7, pltpu.roll(u, 2, axis=0)))
    y = bg_ref[...] * (u2 * w_ref[0:1, :] + u1 * w_ref[1:2, :] + u * w_ref[2:3, :])
    y_ref[...] = y.astype(y_ref.dtype)
    tail = u[tt - SUBLANES:tt, :]
    carry[...] = tail
    tail_ref[0] = tail


def _short_conv(y, w3, buf, *, col_b, col_c, col_h, width, row0, nseq, tb, name):
    tt = _pick_tile(tb, 512)
    nt = tb // tt
    rb0 = row0 // tt

    def cspec(col):
        cb = col // width
        return pl.BlockSpec((tt, width), lambda i, t: (rb0 + i * nt + t, cb))

    return pl.pallas_call(
        _conv_kernel,
        grid=(nseq, nt),
        in_specs=[cspec(col_b), cspec(col_c), cspec(col_h),
                  pl.BlockSpec((3, width), lambda i, t: (0, 0)),
                  pl.BlockSpec((1, 2, width), lambda i, t: (i, 0, 0))],
        out_specs=[pl.BlockSpec((tt, width), lambda i, t: (i * nt + t, 0)),
                   pl.BlockSpec((1, SUBLANES, width), lambda i, t: (i, 0, 0))],
        out_shape=[jax.ShapeDtypeStruct((nseq * tb, width), BF16),
                   jax.ShapeDtypeStruct((nseq, SUBLANES, width), F32)],
        scratch_shapes=[pltpu.VMEM((SUBLANES, width), F32)],
        compiler_params=_params("parallel", "arbitrary"),
        name=name,
    )(y, y, y, w3, buf)


def _attn_kernel(q_ref, k_ref, v_ref, o_ref, kscr, vscr, *, nb, rows, heads, hd):
    scale = hd ** -0.5
    m = max(rows, 2 * SUBLANES)
    mlen = k_ref.shape[1]
    for bi in range(nb):
        k2 = k_ref[bi].reshape(mlen * heads, hd)
        v2 = v_ref[bi].reshape(mlen * heads, hd)
        for c in range(hd // LANES):
            kscr[bi, c] = k2[:, c * LANES:(c + 1) * LANES]
            vscr[bi, c] = v2[:, c * LANES:(c + 1) * LANES]
        qb = _pad_rows(q_ref[bi * rows:(bi + 1) * rows, :], m)
        outs = []
        for hh in range(heads):
            sl = slice(hh * hd, (hh + 1) * hd)
            head_rows = pl.ds(hh, mlen, stride=heads)
            kh = jnp.concatenate([kscr[bi, c, head_rows, :] for c in range(hd // LANES)], axis=1).astype(BF16)
            vh = jnp.concatenate([vscr[bi, c, head_rows, :] for c in range(hd // LANES)], axis=1).astype(BF16)
            s = lax.dot_general(qb[:, sl].astype(BF16), kh, _NT, preferred_element_type=F32) * scale
            p = jnp.exp(s - jnp.max(s, axis=-1, keepdims=True))
            p = p / jnp.sum(p, axis=-1, keepdims=True)
            outs.append(_dot(p.astype(BF16), vh))
        o = jnp.concatenate(outs, axis=-1)
        o_ref[bi * rows:(bi + 1) * rows, :] = o[0:rows].astype(o_ref.dtype)


def _mem_attention(q, mem_k, mem_v, *, kv0, row0, nseq, tb, nb, name):
    d = q.shape[1]
    _, mlen, heads, hd = mem_k.shape
    assert kv0 % nb == 0
    kvb = kv0 // nb
    if nb > 1:
        rows, nt = tb, 1
    else:
        rows = _pick_tile(tb, 512)
        nt = tb // rows
    assert nseq % nb == 0 and row0 % (nb * rows) == 0
    rb0 = row0 // (nb * rows)
    return pl.pallas_call(
        functools.partial(_attn_kernel, nb=nb, rows=rows, heads=heads, hd=hd),
        grid=(nseq // nb, nt),
        in_specs=[pl.BlockSpec((nb * rows, d), lambda i, t: (rb0 + i * nt + t, 0)),
                  pl.BlockSpec((nb, mlen, heads, hd), lambda i, t: (kvb + i, 0, 0, 0)),
                  pl.BlockSpec((nb, mlen, heads, hd), lambda i, t: (kvb + i, 0, 0, 0))],
        out_specs=pl.BlockSpec((nb * rows, d), lambda i, t: (i * nt + t, 0)),
        out_shape=jax.ShapeDtypeStruct((nseq * tb, d), BF16),
        scratch_shapes=[pltpu.VMEM((nb, hd // LANES, mlen * heads, LANES), F32),
                        pltpu.VMEM((nb, hd // LANES, mlen * heads, LANES), F32)],
        compiler_params=_params("parallel", "parallel"),
        name=name,
    )(q, mem_k, mem_v)


def _tile_copy(src_hbm, tok, dst, r, sem, nsub):
    so = tok * nsub if isinstance(tok, int) else pl.multiple_of(tok * nsub, nsub)
    do = pl.multiple_of(r * nsub, nsub)
    return pltpu.make_async_copy(src_hbm.at[pl.ds(so, nsub)], dst.at[pl.ds(do, nsub)], sem)


def _gather_start(src_hbm, idx_ref, n, dst, sem, nsub):
    def body(g, c):
        for u in range(SUBLANES):
            r = g * SUBLANES + u
            _tile_copy(src_hbm, idx_ref[r], dst, r, sem, nsub).start(priority=u % 2)
        return c

    lax.fori_loop(0, n // SUBLANES, body, 0)


def _gather_wait(src_hbm, n, dst, sem, nsub):
    def body(r, c):
        _tile_copy(src_hbm, 0, dst, r, sem, nsub).wait()
        return c

    lax.fori_loop(0, n, body, 0, unroll=8)


def _expert_kernel(te_ref, nu_ref, idx0_ref, idx1_ref, x_hbm, wg_ref, wu_ref, wd_ref, o_ref,
                   xbuf, wgb, wub, wdb, sem, *, d):
    i = pl.program_id(0)
    nu = nu_ref[0]
    nsub = d // LANES

    @pl.when(i == 0)
    def _():
        _gather_start(x_hbm, idx0_ref, EXPERT_TILE, xbuf.at[0], sem.at[0], nsub)

    @pl.when(i + 1 < nu)
    def _():
        nxt = (i + 1) % 2
        _gather_start(x_hbm, idx1_ref, EXPERT_TILE, xbuf.at[nxt], sem.at[nxt], nsub)

    @pl.when(i < nu)
    def _():
        @pl.when((i == 0) | (te_ref[i] != te_ref[jnp.maximum(i - 1, 0)]))
        def _():
            wgb[...] = wg_ref[0].astype(BF16)
            wub[...] = wu_ref[0].astype(BF16)
            wdb[...] = wd_ref[0].astype(BF16)

        slot = i % 2
        _gather_wait(x_hbm, EXPERT_TILE, xbuf.at[slot], sem.at[slot], nsub)
        x = _load_token_tiles(xbuf.at[slot], EXPERT_TILE, d).astype(BF16)
        gate = _dot(x, wgb[...])
        up = _dot(x, wub[...])
        hid = (gate * _sigmoid(gate) * up).astype(BF16)
        _store_token_tiles(o_ref, _dot(hid, wdb[...]))

    @pl.when(i >= nu)
    def _():
        o_ref[...] = jnp.zeros(o_ref.shape, F32)


def _experts(xt, src_tok, tile_expert, n_used, wg, wu, wd):
    d, f = wg.shape[1], wg.shape[2]
    nsub = d // LANES
    p = src_tok.shape[0]
    nt = p // EXPERT_TILE
    live = lambda i, te, nu: jnp.minimum(i, nu[0] - 1)
    wspec = lambda shp: pl.BlockSpec(shp, lambda i, te, nu: (te[live(i, te, nu)], 0, 0))
    grid_spec = pltpu.PrefetchScalarGridSpec(
        num_scalar_prefetch=2,
        grid=(nt,),
        in_specs=[pl.BlockSpec((EXPERT_TILE,), lambda i, te, nu: (i,), memory_space=pltpu.SMEM),
                  pl.BlockSpec((EXPERT_TILE,), lambda i, te, nu: (jnp.minimum(i + 1, nt - 1),), memory_space=pltpu.SMEM),
                  pl.BlockSpec(memory_space=pl.ANY),
                  wspec((1, d, f)), wspec((1, d, f)), wspec((1, f, d))],
        out_specs=pl.BlockSpec((EXPERT_TILE * nsub, LANES), lambda i, te, nu: (i, 0)),
        scratch_shapes=[pltpu.VMEM((2, EXPERT_TILE * nsub, LANES), F32),
                        pltpu.VMEM((d, f), BF16), pltpu.VMEM((d, f), BF16), pltpu.VMEM((f, d), BF16),
                        pltpu.SemaphoreType.DMA((2,))],
    )
    return pl.pallas_call(
        functools.partial(_expert_kernel, d=d),
        grid_spec=grid_spec,
        out_shape=jax.ShapeDtypeStruct((p * nsub, LANES), F32),
        compiler_params=_params("arbitrary"),
        name="moe_experts",
    )(tile_expert, n_used, src_tok, src_tok, xt, wg, wu, wd)


def _combine_ln_kernel(ia0_ref, ia1_ref, ib0_ref, ib1_ref, y_hbm, r_ref, x_ref, g_ref, b_ref, xf_ref, xb_ref,
                       ybuf, sem, *, alpha, nsteps):
    i = pl.program_id(0)
    rows, d = x_ref.shape
    nsub = d // LANES

    def start(idx_a, idx_b, slot):
        _gather_start(y_hbm, idx_a, rows, ybuf.at[slot, 0], sem.at[slot, 0], nsub)
        _gather_start(y_hbm, idx_b, rows, ybuf.at[slot, 1], sem.at[slot, 1], nsub)

    @pl.when(i == 0)
    def _():
        start(ia0_ref, ib0_ref, 0)

    @pl.when(i + 1 < nsteps)
    def _():
        start(ia1_ref, ib1_ref, (i + 1) % 2)

    slot = i % 2
    _gather_wait(y_hbm, rows, ybuf.at[slot, 0], sem.at[slot, 0], nsub)
    _gather_wait(y_hbm, rows, ybuf.at[slot, 1], sem.at[slot, 1], nsub)
    w0 = r_ref[:, 0:1]
    w1 = r_ref[:, 1:2]
    moe = w0 * _load_token_tiles(ybuf.at[slot, 0], rows, d) + w1 * _load_token_tiles(ybuf.at[slot, 1], rows, d)
    y = _layer_norm_rows(alpha * x_ref[...] + moe, g_ref[...], b_ref[...])
    xf_ref[...] = y
    xb_ref[...] = y.astype(BF16)


def _combine_ln(ys, dest_a, dest_b, rinfo, x, g, b, *, alpha, name):
    n, d = x.shape
    nsub = d // LANES
    tm = _pick_tile(n, 512)
    nsteps = n // tm
    row = lambda i: (i, 0)
    fix = lambda i: (0, 0)
    cur = pl.BlockSpec((tm,), lambda i: (i,), memory_space=pltpu.SMEM)
    nxt = pl.BlockSpec((tm,), lambda i: (jnp.minimum(i + 1, nsteps - 1),), memory_space=pltpu.SMEM)
    return pl.pallas_call(
        functools.partial(_combine_ln_kernel, alpha=alpha, nsteps=nsteps),
        grid=(nsteps,),
        in_specs=[cur, nxt, cur, nxt, pl.BlockSpec(memory_space=pl.ANY),
                  pl.BlockSpec((tm, LANES), row), pl.BlockSpec((tm, d), row),
                  pl.BlockSpec((1, d), fix), pl.BlockSpec((1, d), fix)],
        out_specs=[pl.BlockSpec((tm, d), row), pl.BlockSpec((tm, d), row)],
        out_shape=[jax.ShapeDtypeStruct((n, d), F32), jax.ShapeDtypeStruct((n, d), BF16)],
        scratch_shapes=[pltpu.VMEM((2, 2, tm * nsub, LANES), F32), pltpu.SemaphoreType.DMA((2, 2))],
        compiler_params=_params("arbitrary"),
        name=name,
    )(dest_a, dest_a, dest_b, dest_b, ys, rinfo, x, g.reshape(1, d), b.reshape(1, d))


def _dispatch_plan(rinfo, n_exp_total):
    n = rinfo.shape[0]
    a = n * TOP_K
    eid = rinfo[:, 2:2 + TOP_K].astype(jnp.int32).reshape(a)
    onehot = (eid[:, None] == jnp.arange(n_exp_total, dtype=jnp.int32)[None, :]).astype(jnp.int32)
    csum = jnp.cumsum(onehot, axis=0)
    rank = jnp.sum(csum * onehot, axis=1) - 1
    counts = csum[-1]
    padded = ((counts + EXPERT_TILE - 1) // EXPERT_TILE) * EXPERT_TILE
    ends = jnp.cumsum(padded)
    starts = ends - padded
    dest = starts[eid] + rank
    nt = -(-a // EXPERT_TILE) + n_exp_total
    p = nt * EXPERT_TILE
    src_tok = jnp.zeros((p,), jnp.int32).at[dest].set(jnp.arange(a, dtype=jnp.int32) // TOP_K)
    tile_start = jnp.arange(nt, dtype=jnp.int32) * EXPERT_TILE
    tile_expert = jnp.minimum(jnp.sum(tile_start[:, None] >= ends[None, :], axis=1), n_exp_total - 1).astype(jnp.int32)
    n_used = (ends[-1] // EXPERT_TILE).astype(jnp.int32).reshape(1)
    return dest.astype(jnp.int32), src_tok, tile_expert, n_used


def kernel(x_prompt, x_sample, state_hgrn, state_conv, state_gla, cache_mem_k, cache_mem_v, mem_prompt, w_in_a, lb_a, conv_w, gnorm_a, w_out_a, w_in_c, w_gk2, b_gk2, gnorm_c, w_out_c, w_mq, w_mk, w_mv, w_mo, ln_g, ln_b, w_grp, b_grp, w_er, b_er, w_gate, w_up, w_down):
    bsz, seq, d = x_prompt.shape
    dec_b, dec_t, _ = x_sample.shape
    depth = w_mq.shape[0]
    _, _, ha_h, ha_dk, ha_dv = state_hgrn.shape
    _, _, gl_h, gl_dk, gl_dv = state_gla.shape
    _, _, mlen, mem_h, mem_hd = cache_mem_k.shape
    n_groups, n_exp = w_er.shape[1], w_er.shape[3]
    sc_w = conv_w.shape[1]
    sc_k = conv_w.shape[2]
    ha_f, ha_w = ha_h * ha_dk, ha_h * ha_dv
    gl_key, gl_val = gl_h * gl_dk, gl_h * gl_dv
    rank = w_gk2.shape[1]
    alpha = (2.0 * depth) ** 0.25
    assert dec_t <= SAMPLE_ROWS and sc_k == 3 and seq % CHUNK == 0 and rank <= LANES
    assert n_groups + n_groups * n_exp <= LANES

    rows_p = bsz * seq
    rows_s = dec_b * SAMPLE_ROWS
    xs_pad = jnp.pad(x_sample, ((0, 0), (0, SAMPLE_ROWS - dec_t), (0, 0)))
    x = jnp.concatenate([x_prompt.reshape(rows_p, d), xs_pad.reshape(rows_s, d)], axis=0)
    xb = x.astype(BF16)
    consts = _rec_constants()

    mem_b = mem_prompt.astype(BF16)
    mk_p = _mem_kv(mem_b, w_mk.astype(BF16), mem_h, "mem_k_proj")
    mv_p = _mem_kv(mem_b, w_mv.astype(BF16), mem_h, "mem_v_proj")
    ck = cache_mem_k.reshape(depth * dec_b, mlen, mem_h, mem_hd)
    cv = cache_mem_v.reshape(depth * dec_b, mlen, mem_h, mem_hd)
    st_hgrn = state_hgrn.reshape(-1, ha_h, ha_dk, ha_dv)
    st_gla = state_gla.reshape(-1, gl_h, gl_dk, gl_dv)

    lb = jnp.cumsum(jax.nn.softmax(lb_a.astype(F32), axis=0), axis=0)
    lb = lb - lb[0]
    log_lb, log1m_lb, onem_lb = jnp.log(lb), jnp.log1p(-lb), 1.0 - lb

    z_hgrn = jnp.zeros((bsz, ha_h, ha_dk, ha_dv), F32)
    z_conv = jnp.zeros((bsz, sc_k - 1, sc_w), F32)
    z_gla = jnp.zeros((bsz, gl_h, gl_dk, gl_dv), F32)

    hgrn_p, conv_p, gla_p, hgrn_s, conv_s, gla_s = [], [], [], [], [], []
    for l in range(depth):
        j = l // 2
        if l % 2 == 0:
            y = _matmul(xb, w_in_a[j].astype(BF16), name=f"in_proj_{l}")
            extras = (log_lb[j:j + 1], log1m_lb[j:j + 1], onem_lb[j:j + 1])
            cols = dict(mode="hgrn", col_q=0, col_k=ha_f, col_v=2 * ha_f, col_g=2 * ha_f + ha_w)
            o_p, s_p = _recurrence(y, extras, gnorm_a[j], z_hgrn, consts, row0=0, nseq=bsz, tb=seq,
                                   t_valid=seq, hb=ha_h, tblk=_pick_tile(seq, 512), name=f"hgrn_prompt_{l}", **cols)
            o_s, s_s = _recurrence(y, extras, gnorm_a[j], st_hgrn, consts, row0=rows_p, nseq=dec_b, s_off=j * dec_b,
                                   tb=SAMPLE_ROWS, t_valid=dec_t, hb=ha_h, name=f"hgrn_sample_{l}", **cols)
            c0 = 2 * ha_f + 2 * ha_w
            w3 = conv_w[j].T
            ccols = dict(col_b=c0, col_c=c0 + sc_w, col_h=c0 + 2 * sc_w, width=sc_w)
            c_p, t_p = _short_conv(y, w3, z_conv, row0=0, nseq=bsz, tb=seq, name=f"conv_prompt_{l}", **ccols)
            c_s, t_s = _short_conv(y, w3, state_conv[j], row0=rows_p, nseq=dec_b, tb=SAMPLE_ROWS,
                                   name=f"conv_sample_{l}", **ccols)
            hgrn_p.append(s_p)
            hgrn_s.append(s_s)
            conv_p.append(t_p[:, SUBLANES - (sc_k - 1):])
            conv_s.append(t_s[:, dec_t - (sc_k - 1):dec_t])
            acts = [(o_p, o_s), (c_p, c_s)]
            wo = w_out_a[j].astype(BF16)
            weights = [wo[:ha_w], wo[ha_w:]]
        else:
            ncol = 2 * gl_key + 2 * gl_val
            w_in = jnp.pad(w_in_c[j], ((0, 0), (0, LANES - rank))).astype(BF16)
            y = _matmul(xb, w_in, name=f"in_proj_{l}")
            wg2 = jnp.pad(w_gk2[j], ((0, LANES - rank), (0, 0)))
            wgh = wg2.astype(BF16)
            wgl = (wg2 - wgh.astype(F32)).astype(BF16)
            extras = (wgh, wgl, b_gk2[j].reshape(1, gl_key))
            cols = dict(mode="gla", col_q=0, col_k=gl_key, col_v=2 * gl_key, col_g=2 * gl_key + gl_val, col_gz=ncol)
            o_p, s_p = _recurrence(y, extras, gnorm_c[j], z_gla, consts, row0=0, nseq=bsz, tb=seq,
                                   t_valid=seq, hb=gl_h, tblk=_pick_tile(seq, 512), name=f"gla_prompt_{l}", **cols)
            o_s, s_s = _recurrence(y, extras, gnorm_c[j], st_gla, consts, row0=rows_p, nseq=dec_b, s_off=j * dec_b,
                                   tb=SAMPLE_ROWS, t_valid=dec_t, hb=gl_h, name=f"gla_sample_{l}", **cols)
            gla_p.append(s_p)
            gla_s.append(s_s)
            acts = [(o_p, o_s)]
            weights = [w_out_c[j].astype(BF16)]
        x, xb = _proj_ln(acts, weights, x, ln_g[l, 0], ln_b[l, 0], alpha=alpha, name=f"mix_out_ln_{l}")

        q = _matmul(xb, w_mq[l].astype(BF16), name=f"mem_q_{l}")
        a_p = _mem_attention(q, mk_p, mv_p, kv0=l * bsz, row0=0, nseq=bsz, tb=seq, nb=1, name=f"mem_attn_prompt_{l}")
        a_s = _mem_attention(q, ck, cv, kv0=l * dec_b, row0=rows_p, nseq=dec_b, tb=SAMPLE_ROWS,
                             nb=4 if dec_b % 4 == 0 else 1, name=f"mem_attn_sample_{l}")
        wr = jnp.concatenate([w_grp[l]] + [w_er[l, gi] for gi in range(n_groups)], axis=1)
        wr = jnp.pad(wr, ((0, 0), (0, LANES - wr.shape[1])))
        wrh = wr.astype(BF16)
        wrl = (wr - wrh.astype(F32)).astype(BF16)
        br = jnp.concatenate([b_grp[l], b_er[l].reshape(-1)])
        br = jnp.pad(br, (0, LANES - br.shape[0])).reshape(1, LANES)
        x, xb, rinfo, xt = _proj_ln([(a_p, a_s)], [w_mo[l].astype(BF16)], x, ln_g[l, 1],
                                    ln_b[l, 1], alpha=alpha, router_w=(wrh, wrl, br), router=(n_groups, n_exp),
                                    name=f"mem_out_ln_router_{l}")

        ne = n_groups * n_exp
        nsub = d // LANES
        dest, src_tok, tile_expert, n_used = _dispatch_plan(rinfo, ne)
        f = w_gate.shape[-1]
        ys = _experts(xt, src_tok, tile_expert + l * ne, n_used, w_gate.reshape(depth * ne, d, f),
                      w_up.reshape(depth * ne, d, f), w_down.reshape(depth * ne, f, d))
        x, xb = _combine_ln(ys, dest[0::TOP_K], dest[1::TOP_K], rinfo, x, ln_g[l, 2], ln_b[l, 2],
                            alpha=alpha, name=f"moe_combine_ln_{l}")

    y_prompt = x[:rows_p].reshape(bsz, seq, d)
    y_sample = x[rows_p:].reshape(dec_b, SAMPLE_ROWS, d)[:, :dec_t]
    return (y_prompt, y_sample, jnp.stack(hgrn_p), jnp.stack(conv_p), jnp.stack(gla_p),
            mk_p.reshape(depth, bsz, mlen, mem_h, mem_hd), mv_p.reshape(depth, bsz, mlen, mem_h, mem_hd),
            jnp.stack(hgrn_s), jnp.stack(conv_s), jnp.stack(gla_s))
```

```python
import functools
import math

import numpy as np
import jax
import jax.numpy as jnp
from jax import lax
from jax.experimental import pallas as pl
from jax.experimental.pallas import tpu as pltpu

F32 = jnp.float32
BF16 = jnp.bfloat16

LN_EPS = 1e-5
RMS_EPS = 1e-6
GLA_NORMALIZER = 16.0
TOP_K = 2

LANES = 128
SUBLANES = 8
CHUNK = 128
SAMPLE_ROWS = 8
EXPERT_TILE = 256
EXPERT_LOOKAHEAD = 2
VMEM_LIMIT = 48 * 1024 * 1024

_NT = (((1,), (1,)), ((), ()))
_TN = (((0,), (0,)), ((), ()))


def _params(*sem):
    return pltpu.CompilerParams(dimension_semantics=sem, vmem_limit_bytes=VMEM_LIMIT)


def _pick_tile(n, pref):
    t = pref
    while t > SUBLANES and n % t:
        t //= 2
    assert n % t == 0, (n, pref)
    return t


def _dot(a, b):
    return jnp.dot(a, b, preferred_element_type=F32)


def _sigmoid(x):
    return 1.0 / (1.0 + jnp.exp(-x))


def _log_sigmoid(x):
    return jnp.minimum(x, 0.0) - jnp.log1p(jnp.exp(-jnp.abs(x)))


def _split2(a):
    hi = a.astype(BF16)
    lo = (a - hi.astype(F32)).astype(BF16)
    return hi, lo


def _split3(a):
    hi = a.astype(BF16)
    r = a - hi.astype(F32)
    mid = r.astype(BF16)
    lo = (r - mid.astype(F32)).astype(BF16)
    return hi, mid, lo


def _pad_rows(x, n):
    if x.shape[0] == n:
        return x
    return jnp.concatenate([x, jnp.zeros((n - x.shape[0], x.shape[1]), x.dtype)], axis=0)


def _store_token_tiles(ref, y):
    rows, d = y.shape
    nsub = d // LANES
    for s in range(nsub):
        ref[pl.ds(s, rows, stride=nsub), :] = y[:, s * LANES:(s + 1) * LANES]


def _load_token_tiles(ref, rows, d, row0=0):
    nsub = d // LANES
    return jnp.concatenate([ref[pl.ds(row0 * nsub + s, rows, stride=nsub), :] for s in range(nsub)], axis=1)


def _mm_kernel(x_ref, w_ref, o_ref):
    o_ref[...] = _dot(x_ref[...], w_ref[...]).astype(o_ref.dtype)


def _col_tile(m, pref=1024):
    return max(t for t in range(LANES, pref + 1, LANES) if m % t == 0)


def _matmul(x, w, *, out_dtype=F32, name):
    n, k = x.shape
    m = w.shape[1]
    tm = _pick_tile(n, 1024)
    tn = _col_tile(m)
    return pl.pallas_call(
        _mm_kernel,
        grid=(n // tm, m // tn),
        in_specs=[pl.BlockSpec((tm, k), lambda i, j: (i, 0)),
                  pl.BlockSpec((k, tn), lambda i, j: (0, j))],
        out_specs=pl.BlockSpec((tm, tn), lambda i, j: (i, j)),
        out_shape=jax.ShapeDtypeStruct((n, m), out_dtype),
        compiler_params=_params("parallel", "parallel"),
        name=name,
    )(x, w)


def _mem_kv_kernel(x_ref, w_ref, o_ref, *, heads):
    y = _dot(x_ref[0], w_ref[0])
    hd = y.shape[1] // heads
    for hh in range(heads):
        o_ref[0, :, hh, :] = y[:, hh * hd:(hh + 1) * hd]


def _mem_kv(mem_bf16, w_all, heads, name):
    nb, mlen, d = mem_bf16.shape
    nj, _, dout = w_all.shape
    return pl.pallas_call(
        functools.partial(_mem_kv_kernel, heads=heads),
        grid=(nj, nb),
        in_specs=[pl.BlockSpec((1, mlen, d), lambda j, i: (i, 0, 0)),
                  pl.BlockSpec((1, d, dout), lambda j, i: (j, 0, 0))],
        out_specs=pl.BlockSpec((1, mlen, heads, dout // heads), lambda j, i: (j * nb + i, 0, 0, 0)),
        out_shape=jax.ShapeDtypeStruct((nj * nb, mlen, heads, dout // heads), F32),
        compiler_params=_params("parallel", "parallel"),
        name=name,
    )(mem_bf16, w_all)


def _layer_norm_rows(z, g, b):
    mu = jnp.mean(z, axis=-1, keepdims=True)
    zc = z - mu
    var = jnp.mean(zc * zc, axis=-1, keepdims=True)
    return zc * lax.rsqrt(var + LN_EPS) * g + b


def _route(logits, n_groups, n_exp):
    lane = lax.broadcasted_iota(jnp.int32, logits.shape, 1).astype(F32)
    big = jnp.float32(1 << 20)
    neg = jnp.float32(-jnp.inf)
    gl = jnp.where(lane < n_groups, logits, neg)
    gmax = jnp.max(gl, axis=-1, keepdims=True)
    gsel = jnp.min(jnp.where(gl == gmax, lane, big), axis=-1, keepdims=True)
    gden = jnp.sum(jnp.exp(gl - gmax), axis=-1, keepdims=True)
    g_w = 1.0 / gden
    lo = n_groups + gsel * n_exp
    el = jnp.where((lane >= lo) & (lane < lo + n_exp), logits, neg)
    m1 = jnp.max(el, axis=-1, keepdims=True)
    i1 = jnp.min(jnp.where(el == m1, lane, big), axis=-1, keepdims=True)
    el2 = jnp.where(lane == i1, neg, el)
    m2 = jnp.max(el2, axis=-1, keepdims=True)
    i2 = jnp.min(jnp.where(el2 == m2, lane, big), axis=-1, keepdims=True)
    t = jnp.exp(m2 - m1)
    w1 = g_w / (1.0 + t)
    w2 = g_w * t / (1.0 + t)
    e1 = i1 - n_groups
    e2 = i2 - n_groups
    zero = jnp.zeros(logits.shape, F32)
    return jnp.where(lane == 0, w1, jnp.where(lane == 1, w2, jnp.where(lane == 2, e1, jnp.where(lane == 3, e2, zero))))


def _proj_ln_kernel(*refs, n_in, alpha, router, prompt_tiles):
    ap_refs = refs[:n_in]
    as_refs = refs[n_in:2 * n_in]
    w_refs = refs[2 * n_in:3 * n_in]
    x_ref, g_ref, b_ref = refs[3 * n_in:3 * n_in + 3]
    pos = 3 * n_in + 3
    if router is not None:
        wrh_ref, wrl_ref, br_ref = refs[pos:pos + 3]
        pos += 3
    xf_ref, xb_ref = refs[pos:pos + 2]
    is_prompt = pl.program_id(0) < prompt_tiles
    acc = None
    for ap_ref, as_ref, w_ref in zip(ap_refs, as_refs, w_refs):
        part = _dot(jnp.where(is_prompt, ap_ref[...], as_ref[...]), w_ref[...])
        acc = part if acc is None else acc + part
    y = _layer_norm_rows(alpha * x_ref[...] + acc, g_ref[...], b_ref[...])
    xf_ref[...] = y
    xb_ref[...] = y.astype(BF16)
    if router is not None:
        hi, lo = _split2(y)
        logits = _dot(hi, wrh_ref[...]) + _dot(lo, wrh_ref[...]) + _dot(hi, wrl_ref[...]) + br_ref[...]
        refs[pos + 2][...] = _route(logits, *router)
        _store_token_tiles(refs[pos + 3], y)


def _proj_ln(acts, weights, x, g, b, *, alpha, router_w=None, router=None, name):
    n, d = x.shape
    n_p = acts[0][0].shape[0]
    tm = _pick_tile(math.gcd(n_p, n - n_p), 512)
    npt = n_p // tm
    row = lambda i: (i, 0)
    fix = lambda i: (0, 0)
    in_specs = [pl.BlockSpec((tm, a[0].shape[1]), lambda i: (jnp.minimum(i, npt - 1), 0)) for a in acts]
    in_specs += [pl.BlockSpec((tm, a[1].shape[1]), lambda i: (jnp.maximum(i - npt, 0), 0)) for a in acts]
    in_specs += [pl.BlockSpec(w.shape, fix) for w in weights]
    in_specs += [pl.BlockSpec((tm, d), row), pl.BlockSpec((1, d), fix), pl.BlockSpec((1, d), fix)]
    args = [a[0] for a in acts] + [a[1] for a in acts] + list(weights) + [x, g.reshape(1, d), b.reshape(1, d)]
    out_specs = [pl.BlockSpec((tm, d), row), pl.BlockSpec((tm, d), row)]
    out_shape = [jax.ShapeDtypeStruct((n, d), F32), jax.ShapeDtypeStruct((n, d), BF16)]
    if router_w is not None:
        in_specs += [pl.BlockSpec(r.shape, fix) for r in router_w]
        args += list(router_w)
        out_specs.append(pl.BlockSpec((tm, LANES), row))
        out_shape.append(jax.ShapeDtypeStruct((n, LANES), F32))
        out_specs.append(pl.BlockSpec((tm * (d // LANES), LANES), row))
        out_shape.append(jax.ShapeDtypeStruct((n * (d // LANES), LANES), F32))
    return pl.pallas_call(
        functools.partial(_proj_ln_kernel, n_in=len(acts), alpha=alpha, router=router, prompt_tiles=npt),
        grid=(n // tm,),
        in_specs=in_specs, out_specs=out_specs, out_shape=out_shape,
        compiler_params=_params("parallel"),
        name=name,
    )(*args)


_LEVELS = (8, 16, 32, 64)


def _rec_constants():
    t = np.arange(CHUNK)
    tri = (t[:, None] >= t[None, :]).astype(np.float32)
    masks = []
    for lvl in _LEVELS:
        bt, bs = t[:, None] // lvl, t[None, :] // lvl
        masks.append(((bt - bs == 1) & (bt % 2 == 1)).astype(np.float32))
    return jnp.asarray(tri, BF16), jnp.asarray(np.stack(masks), F32)


def _rec_chunk(q, k, a, v, st, tri_ref, mask_ref, nv):
    r = SUBLANES * nv
    m = max(r, 2 * SUBLANES)
    dk = q.shape[1]

    hi, mid, lo = _split3(_pad_rows(a, CHUNK))
    tri = tri_ref[0:m, :]
    b = (_dot(tri, hi) + _dot(tri, mid) + _dot(tri, lo))[0:r]
    b_last = b[r - 1:r]

    scores = None
    for li, lvl in enumerate(_LEVELS):
        if lvl >= r:
            continue
        qp, kp = [], []
        for blk in range(r // lvl):
            rows = slice(blk * lvl, (blk + 1) * lvl)
            if blk % 2 == 1:
                rho = b[blk * lvl - 1:blk * lvl]
                qp.append(q[rows] * jnp.exp(b[rows] - rho))
                kp.append(jnp.zeros((lvl, dk), F32))
            else:
                rho = b[(blk + 1) * lvl - 1:(blk + 1) * lvl]
                kp.append(k[rows] * jnp.exp(rho - b[rows]))
                qp.append(jnp.zeros((lvl, dk), F32))
        qh = jnp.concatenate(qp, axis=0).astype(BF16)
        kh = jnp.concatenate(kp, axis=0).astype(BF16)
        s = lax.dot_general(qh, kh, _NT, preferred_element_type=F32) * mask_ref[li]
        scores = s if scores is None else scores + s

    lane = lax.broadcasted_iota(jnp.int32, (SUBLANES, CHUNK), 1)
    sub = lax.broadcasted_iota(jnp.int32, (SUBLANES, CHUNK), 0)
    dparts = []
    for j in range(nv):
        rows = slice(SUBLANES * j, SUBLANES * (j + 1))
        qj, kj, bj = q[rows], k[rows], b[rows]
        acc = jnp.zeros((SUBLANES, CHUNK), F32)
        for s_ in range(SUBLANES):
            e = jnp.exp(jnp.minimum(bj - bj[s_:s_ + 1], 0.0))
            col = jnp.sum(qj * kj[s_:s_ + 1] * e, axis=1, keepdims=True)
            acc = jnp.where(lane == SUBLANES * j + s_, col, acc)
        dparts.append(jnp.where(lane <= sub + SUBLANES * j, acc, 0.0))
    diag = _pad_rows(jnp.concatenate(dparts, axis=0) if nv > 1 else dparts[0], m)
    scores = diag if scores is None else scores + diag

    v_bf = _pad_rows(v, CHUNK).astype(BF16)
    o = _dot(scores.astype(BF16), v_bf)
    qe = _pad_rows(q * jnp.exp(b), m).astype(BF16)
    o = o + lax.dot_general(qe, st.astype(BF16), _NT, preferred_element_type=F32)
    kd = _pad_rows(k * jnp.exp(b_last - b), CHUNK).astype(BF16)
    st_new = st * jnp.exp(b_last) + lax.dot_general(v_bf, kd, _TN, preferred_element_type=F32)
    return o[0:r], st_new


def _rec_kernel(*refs, mode, hb, sb, dk, dv, nv, nchunks, t_valid, qscale):
    q_ref, k_ref, v_ref, g_ref = refs[:4]
    pos = 4
    if mode == "gla":
        gz_ref, wgh_ref, wgl_ref, bg_ref = refs[pos:pos + 4]
        pos += 4
    else:
        loglb_ref, log1m_ref, onem_ref = refs[pos:pos + 3]
        pos += 3
    nw_ref, s0_ref, tri_ref, mask_ref = refs[pos:pos + 4]
    o_ref, so_ref, st_scr = refs[pos + 4:pos + 7]
    r = SUBLANES * nv
    tstep = pl.program_id(2)

    @pl.when(tstep == 0)
    def _():
        for sq in range(sb):
            for hh in range(hb):
                st_scr[sq * hb + hh] = s0_ref[sq, hh].T

    def chunk_step(c, carry, sq=0):
        rsl = pl.ds(c * r if isinstance(c, int) else pl.multiple_of(c * r, r), r)
        if t_valid < r:
            valid_k = lax.broadcasted_iota(jnp.int32, (r, dk), 0) < t_valid
            valid_v = lax.broadcasted_iota(jnp.int32, (r, dv), 0) < t_valid
        if mode == "gla":
            gz = _pad_rows(gz_ref[rsl, :], max(r, 2 * SUBLANES))
            gzh, gzl = _split2(gz)
        for hh in range(hb):
            ksl = slice(hh * dk, (hh + 1) * dk)
            vsl = slice(hh * dv, (hh + 1) * dv)
            qr = q_ref[rsl, ksl]
            kr = k_ref[rsl, ksl]
            v = v_ref[rsl, vsl]
            if mode == "gla":
                gk = (_dot(gzh, wgh_ref[:, ksl]) + _dot(gzl, wgh_ref[:, ksl]) + _dot(gzh, wgl_ref[:, ksl]))[0:r]
                a = _log_sigmoid(gk + bg_ref[:, ksl]) / GLA_NORMALIZER
                q = qr * qscale
                k = kr
            else:
                ls = _log_sigmoid(kr)
                x2 = log1m_ref[:, ksl] + ls
                x1 = loglb_ref[:, ksl]
                a = jnp.maximum(x1, x2) + jnp.log1p(jnp.exp(-jnp.abs(x1 - x2)))
                k = onem_ref[:, ksl] * _sigmoid(-kr)
                q = qr * _sigmoid(qr)
            if t_valid < r:
                a = jnp.where(valid_k, a, 0.0)
                k = jnp.where(valid_k, k, 0.0)
                v = jnp.where(valid_v, v, 0.0)
            o, st_new = _rec_chunk(q, k, a, v, st_scr[sq * hb + hh], tri_ref, mask_ref, nv)
            st_scr[sq * hb + hh] = st_new
            g = g_ref[rsl, vsl]
            y = o * lax.rsqrt(jnp.mean(o * o, axis=-1, keepdims=True) + RMS_EPS) * nw_ref[...]
            o_ref[rsl, vsl] = (y * (g * _sigmoid(g))).astype(o_ref.dtype)
        return carry

    if nchunks == 1:
        for sq in range(sb):
            chunk_step(sq, 0, sq)
    else:
        lax.fori_loop(0, nchunks, chunk_step, 0)

    @pl.when(tstep == pl.num_programs(2) - 1)
    def _():
        for sq in range(sb):
            for hh in range(hb):
                so_ref[sq, hh] = st_scr[sq * hb + hh].T


def _recurrence(y, extras, norm_w, s0, consts, *, mode, col_q, col_k, col_v, col_g, col_gz=None,
                row0, nseq, tb, t_valid, hb, name, s_off=0, tblk=None, sb=1):
    _, h, dk, dv = s0.shape
    tblk = tb if tblk is None else tblk
    nt = tb // tblk
    nv = min(tblk, CHUNK) // SUBLANES
    nchunks = tblk // (SUBLANES * nv)
    assert h % hb == 0 and tb % tblk == 0 and nseq % sb == 0 and s_off % sb == 0
    assert sb == 1 or (nt == 1 and nchunks == 1)
    rblk = sb * tblk
    assert row0 % rblk == 0
    rb0 = row0 // rblk
    sblk0 = s_off // sb
    kw, vw = hb * dk, hb * dv
    tri, masks = consts

    def cspec(width, col):
        assert col % width == 0
        cb = col // width
        return pl.BlockSpec((rblk, width), lambda i, j, t: (rb0 + i * nt + t, cb + j))

    fixh = lambda width: pl.BlockSpec((1, width), lambda i, j, t: (0, j))
    in_specs = [cspec(kw, col_q), cspec(kw, col_k), cspec(vw, col_v), cspec(vw, col_g)]
    args = [y, y, y, y]
    if mode == "gla":
        wgh, wgl, bg = extras
        in_specs += [pl.BlockSpec((rblk, LANES), lambda i, j, t: (rb0 + i * nt + t, col_gz // LANES)),
                     pl.BlockSpec((LANES, kw), lambda i, j, t: (0, j)),
                     pl.BlockSpec((LANES, kw), lambda i, j, t: (0, j)), fixh(kw)]
        args += [y, wgh, wgl, bg]
    else:
        in_specs += [fixh(kw), fixh(kw), fixh(kw)]
        args += list(extras)
    in_specs += [pl.BlockSpec((1, dv), lambda i, j, t: (0, 0)),
                 pl.BlockSpec((sb, hb, dk, dv), lambda i, j, t: (sblk0 + i, j, 0, 0)),
                 pl.BlockSpec(tri.shape, lambda i, j, t: (0, 0)),
                 pl.BlockSpec(masks.shape, lambda i, j, t: (0, 0, 0))]
    args += [norm_w.reshape(1, dv), s0, tri, masks]
    return pl.pallas_call(
        functools.partial(_rec_kernel, mode=mode, hb=hb, sb=sb, dk=dk, dv=dv, nv=nv, nchunks=nchunks,
                          t_valid=t_valid, qscale=dk ** -0.5),
        grid=(nseq // sb, h // hb, nt),
        in_specs=in_specs,
        out_specs=[pl.BlockSpec((rblk, vw), lambda i, j, t: (i * nt + t, j)),
                   pl.BlockSpec((sb, hb, dk, dv), lambda i, j, t: (i, j, 0, 0))],
        out_shape=[jax.ShapeDtypeStruct((nseq * tb, h * dv), BF16),
                   jax.ShapeDtypeStruct((nseq, h, dk, dv), F32)],
        scratch_shapes=[pltpu.VMEM((sb * hb, dv, dk), F32)],
        compiler_params=_params("parallel", "parallel", "arbitrary"),
        name=name,
    )(*args)


def _conv_kernel(bg_ref, cg_ref, h_ref, w_ref, buf_ref, y_ref, tail_ref, carry):
    t = pl.program_id(1)
    u = cg_ref[...] * h_ref[...]
    tt = u.shape[0]

    @pl.when(t == 0)
    def _():
        carry[...] = jnp.zeros(carry.shape, F32)
        carry[SUBLANES - 2:SUBLANES, :] = buf_ref[0]

    p6 = carry[SUBLANES - 2:SUBLANES - 1, :]
    p7 = carry[SUBLANES - 1:SUBLANES, :]
    row = lax.broadcasted_iota(jnp.int32, u.shape, 0)
    u1 = jnp.where(row == 0, p7, pltpu.roll(u, 1, axis=0))
    u2 = jnp.where(row == 0, p6, jnp.where(row == 1, p7, pltpu.roll(u, 2, axis=0)))
    y = bg_ref[...] * (u2 * w_ref[0:1, :] + u1 * w_ref[1:2, :] + u * w_ref[2:3, :])
    y_ref[...] = y.astype(y_ref.dtype)
    tail = u[tt - SUBLANES:tt, :]
    carry[...] = tail
    tail_ref[0] = tail


def _short_conv(y, w3, buf, *, col_b, col_c, col_h, width, row0, nseq, tb, name):
    tt = _pick_tile(tb, 512)
    nt = tb // tt
    rb0 = row0 // tt

    def cspec(col):
        cb = col // width
        return pl.BlockSpec((tt, width), lambda i, t: (rb0 + i * nt + t, cb))

    return pl.pallas_call(
        _conv_kernel,
        grid=(nseq, nt),
        in_specs=[cspec(col_b), cspec(col_c), cspec(col_h),
                  pl.BlockSpec((3, width), lambda i, t: (0, 0)),
                  pl.BlockSpec((1, 2, width), lambda i, t: (i, 0, 0))],
        out_specs=[pl.BlockSpec((tt, width), lambda i, t: (i * nt + t, 0)),
                   pl.BlockSpec((1, SUBLANES, width), lambda i, t: (i, 0, 0))],
        out_shape=[jax.ShapeDtypeStruct((nseq * tb, width), BF16),
                   jax.ShapeDtypeStruct((nseq, SUBLANES, width), F32)],
        scratch_shapes=[pltpu.VMEM((SUBLANES, width), F32)],
        compiler_params=_params("parallel", "arbitrary"),
        name=name,
    )(y, y, y, w3, buf)


def _attn_kernel(q_ref, k_ref, v_ref, o_ref, kscr, vscr, *, nb, rows, heads, hd):
    scale = hd ** -0.5
    m = max(rows, 2 * SUBLANES)
    mlen = k_ref.shape[1]
    for bi in range(nb):
        k2 = k_ref[bi].reshape(mlen * heads, hd)
        v2 = v_ref[bi].reshape(mlen * heads, hd)
        for c in range(hd // LANES):
            kscr[bi, c] = k2[:, c * LANES:(c + 1) * LANES]
            vscr[bi, c] = v2[:, c * LANES:(c + 1) * LANES]
        qb = _pad_rows(q_ref[bi * rows:(bi + 1) * rows, :], m)
        outs = []
        for hh in range(heads):
            sl = slice(hh * hd, (hh + 1) * hd)
            head_rows = pl.ds(hh, mlen, stride=heads)
            kh = jnp.concatenate([kscr[bi, c, head_rows, :] for c in range(hd // LANES)], axis=1).astype(BF16)
            vh = jnp.concatenate([vscr[bi, c, head_rows, :] for c in range(hd // LANES)], axis=1).astype(BF16)
            s = lax.dot_general(qb[:, sl].astype(BF16), kh, _NT, preferred_element_type=F32) * scale
            p = jnp.exp(s - jnp.max(s, axis=-1, keepdims=True))
            p = p / jnp.sum(p, axis=-1, keepdims=True)
            outs.append(_dot(p.astype(BF16), vh))
        o = jnp.concatenate(outs, axis=-1)
        o_ref[bi * rows:(bi + 1) * rows, :] = o[0:rows].astype(o_ref.dtype)


def _attn_short_kernel(q_ref, k_ref, v_ref, o_ref, *, nb, rows, heads, hd):
    scale = hd ** -0.5
    mlen = k_ref.shape[1]
    nrow, ncol = heads * rows, mlen * heads
    col_head = lax.broadcasted_iota(jnp.int32, (nrow, ncol), 1) & (heads - 1)
    row_head = lax.shift_right_logical(lax.broadcasted_iota(jnp.int32, (nrow, ncol), 0),
                                       jnp.int32(rows.bit_length() - 1))
    own = col_head == row_head
    for bi in range(nb):
        k2 = k_ref[bi].reshape(ncol, hd).astype(BF16)
        v2 = v_ref[bi].reshape(ncol, hd).astype(BF16)
        qf = q_ref[bi * rows:(bi + 1) * rows, :]
        qs = jnp.concatenate([qf[:, hh * hd:(hh + 1) * hd] for hh in range(heads)], axis=0).astype(BF16)
        s = lax.dot_general(qs, k2, _NT, preferred_element_type=F32) * scale
        s = jnp.where(own, s, -jnp.inf)
        p = jnp.exp(s - jnp.max(s, axis=-1, keepdims=True))
        p = p / jnp.sum(p, axis=-1, keepdims=True)
        o2 = _dot(p.astype(BF16), v2)
        o = jnp.concatenate([o2[hh * rows:(hh + 1) * rows] for hh in range(heads)], axis=1)
        o_ref[bi * rows:(bi + 1) * rows, :] = o.astype(o_ref.dtype)


def _mem_attention(q, mem_k, mem_v, *, kv0, row0, nseq, tb, nb, name):
    d = q.shape[1]
    _, mlen, heads, hd = mem_k.shape
    assert kv0 % nb == 0
    kvb = kv0 // nb
    if nb > 1:
        rows, nt = tb, 1
    else:
        rows = _pick_tile(tb, 512)
        nt = tb // rows
    assert nseq % nb == 0 and row0 % (nb * rows) == 0
    rb0 = row0 // (nb * rows)
    short = nb > 1 and rows % SUBLANES == 0 and heads & (heads - 1) == 0 and rows & (rows - 1) == 0
    scratch = [] if short else [pltpu.VMEM((nb, hd // LANES, mlen * heads, LANES), F32)] * 2
    return pl.pallas_call(
        functools.partial(_attn_short_kernel if short else _attn_kernel, nb=nb, rows=rows, heads=heads, hd=hd),
        grid=(nseq // nb, nt),
        in_specs=[pl.BlockSpec((nb * rows, d), lambda i, t: (rb0 + i * nt + t, 0)),
                  pl.BlockSpec((nb, mlen, heads, hd), lambda i, t: (kvb + i, 0, 0, 0)),
                  pl.BlockSpec((nb, mlen, heads, hd), lambda i, t: (kvb + i, 0, 0, 0))],
        out_specs=pl.BlockSpec((nb * rows, d), lambda i, t: (i * nt + t, 0)),
        out_shape=jax.ShapeDtypeStruct((nseq * tb, d), BF16),
        scratch_shapes=scratch,
        compiler_params=_params("parallel", "parallel"),
        name=name,
    )(q, mem_k, mem_v)


def _tile_copy(src_hbm, tok, dst, r, sem, nsub):
    so = tok * nsub if isinstance(tok, int) else pl.multiple_of(tok * nsub, nsub)
    do = pl.multiple_of(r * nsub, nsub)
    return pltpu.make_async_copy(src_hbm.at[pl.ds(so, nsub)], dst.at[pl.ds(do, nsub)], sem)


def _gather_start(src_hbm, idx_ref, n, dst, sem, nsub):
    def body(g, c):
        for u in range(SUBLANES):
            r = g * SUBLANES + u
            _tile_copy(src_hbm, idx_ref[r], dst, r, sem, nsub).start(priority=u % 2)
        return c

    lax.fori_loop(0, n // SUBLANES, body, 0)


def _gather_wait(src_hbm, n, dst, sem, nsub):
    def body(r, c):
        _tile_copy(src_hbm, 0, dst, r, sem, nsub).wait()
        return c

    lax.fori_loop(0, n, body, 0, unroll=8)


def _expert_kernel(te_ref, nu_ref, *refs, d):
    depth = EXPERT_LOOKAHEAD
    idx_refs = refs[:depth + 1]
    x_hbm, wg_ref, wu_ref, wd_ref, o_ref, xbuf, wgb, wub, wdb, sem = refs[depth + 1:]
    i = pl.program_id(0)
    nu = nu_ref[0]
    nsub = d // LANES

    for k in range(depth):
        @pl.when((i == 0) & (k < nu))
        def _(k=k):
            _gather_start(x_hbm, idx_refs[k], EXPERT_TILE, xbuf.at[k], sem.at[k], nsub)

    @pl.when(i + depth < nu)
    def _():
        nxt = (i + depth) % (depth + 1)
        _gather_start(x_hbm, idx_refs[depth], EXPERT_TILE, xbuf.at[nxt], sem.at[nxt], nsub)

    @pl.when(i < nu)
    def _():
        @pl.when((i == 0) | (te_ref[i] != te_ref[jnp.maximum(i - 1, 0)]))
        def _():
            wgb[...] = wg_ref[0].astype(BF16)
            wub[...] = wu_ref[0].astype(BF16)
            wdb[...] = wd_ref[0].astype(BF16)

        slot = i % (depth + 1)
        _gather_wait(x_hbm, EXPERT_TILE, xbuf.at[slot], sem.at[slot], nsub)
        x = _load_token_tiles(xbuf.at[slot], EXPERT_TILE, d).astype(BF16)
        gate = _dot(x, wgb[...])
        up = _dot(x, wub[...])
        hid = (gate * _sigmoid(gate) * up).astype(BF16)
        _store_token_tiles(o_ref, _dot(hid, wdb[...]))

    @pl.when(i >= nu)
    def _():
        o_ref[...] = jnp.zeros(o_ref.shape, F32)


def _experts(xt, src_tok, tile_expert, n_used, wg, wu, wd):
    d, f = wg.shape[1], wg.shape[2]
    nsub = d // LANES
    p = src_tok.shape[0]
    nt = p // EXPERT_TILE
    live = lambda i, te, nu: jnp.minimum(i, nu[0] - 1)
    wspec = lambda shp: pl.BlockSpec(shp, lambda i, te, nu: (te[live(i, te, nu)], 0, 0))
    depth = EXPERT_LOOKAHEAD
    idx_spec = lambda k: pl.BlockSpec((EXPERT_TILE,), lambda i, te, nu: (jnp.minimum(i + k, nt - 1),),
                                      memory_space=pltpu.SMEM)
    grid_spec = pltpu.PrefetchScalarGridSpec(
        num_scalar_prefetch=2,
        grid=(nt,),
        in_specs=[idx_spec(k) for k in range(depth + 1)]
        + [pl.BlockSpec(memory_space=pl.ANY), wspec((1, d, f)), wspec((1, d, f)), wspec((1, f, d))],
        out_specs=pl.BlockSpec((EXPERT_TILE * nsub, LANES), lambda i, te, nu: (i, 0)),
        scratch_shapes=[pltpu.VMEM((depth + 1, EXPERT_TILE * nsub, LANES), F32),
                        pltpu.VMEM((d, f), BF16), pltpu.VMEM((d, f), BF16), pltpu.VMEM((f, d), BF16),
                        pltpu.SemaphoreType.DMA((depth + 1,))],
    )
    return pl.pallas_call(
        functools.partial(_expert_kernel, d=d),
        grid_spec=grid_spec,
        out_shape=jax.ShapeDtypeStruct((p * nsub, LANES), F32),
        compiler_params=_params("arbitrary"),
        name="moe_experts",
    )(tile_expert, n_used, *([src_tok] * (depth + 1)), xt, wg, wu, wd)


def _combine_ln_kernel(ia0_ref, ia1_ref, ib0_ref, ib1_ref, y_hbm, r_ref, x_ref, g_ref, b_ref, xf_ref, xb_ref,
                       ybuf, sem, *, alpha, nsteps):
    i = pl.program_id(0)
    rows, d = x_ref.shape
    nsub = d // LANES

    def start(idx_a, idx_b, slot):
        _gather_start(y_hbm, idx_a, rows, ybuf.at[slot, 0], sem.at[slot, 0], nsub)
        _gather_start(y_hbm, idx_b, rows, ybuf.at[slot, 1], sem.at[slot, 1], nsub)

    @pl.when(i == 0)
    def _():
        start(ia0_ref, ib0_ref, 0)

    @pl.when(i + 1 < nsteps)
    def _():
        start(ia1_ref, ib1_ref, (i + 1) % 2)

    slot = i % 2
    _gather_wait(y_hbm, rows, ybuf.at[slot, 0], sem.at[slot, 0], nsub)
    _gather_wait(y_hbm, rows, ybuf.at[slot, 1], sem.at[slot, 1], nsub)
    w0 = r_ref[:, 0:1]
    w1 = r_ref[:, 1:2]
    moe = w0 * _load_token_tiles(ybuf.at[slot, 0], rows, d) + w1 * _load_token_tiles(ybuf.at[slot, 1], rows, d)
    y = _layer_norm_rows(alpha * x_ref[...] + moe, g_ref[...], b_ref[...])
    xf_ref[...] = y
    xb_ref[...] = y.astype(BF16)


def _combine_ln(ys, dest_a, dest_b, rinfo, x, g, b, *, alpha, name):
    n, d = x.shape
    nsub = d // LANES
    tm = _pick_tile(n, 512)
    nsteps = n // tm
    row = lambda i: (i, 0)
    fix = lambda i: (0, 0)
    cur = pl.BlockSpec((tm,), lambda i: (i,), memory_space=pltpu.SMEM)
    nxt = pl.BlockSpec((tm,), lambda i: (jnp.minimum(i + 1, nsteps - 1),), memory_space=pltpu.SMEM)
    return pl.pallas_call(
        functools.partial(_combine_ln_kernel, alpha=alpha, nsteps=nsteps),
        grid=(nsteps,),
        in_specs=[cur, nxt, cur, nxt, pl.BlockSpec(memory_space=pl.ANY),
                  pl.BlockSpec((tm, LANES), row), pl.BlockSpec((tm, d), row),
                  pl.BlockSpec((1, d), fix), pl.BlockSpec((1, d), fix)],
        out_specs=[pl.BlockSpec((tm, d), row), pl.BlockSpec((tm, d), row)],
        out_shape=[jax.ShapeDtypeStruct((n, d), F32), jax.ShapeDtypeStruct((n, d), BF16)],
        scratch_shapes=[pltpu.VMEM((2, 2, tm * nsub, LANES), F32), pltpu.SemaphoreType.DMA((2, 2))],
        compiler_params=_params("arbitrary"),
        name=name,
    )(dest_a, dest_a, dest_b, dest_b, ys, rinfo, x, g.reshape(1, d), b.reshape(1, d))


def _dispatch_plan(rinfo, n_exp_total):
    n = rinfo.shape[0]
    a = n * TOP_K
    eid = rinfo[:, 2:2 + TOP_K].astype(jnp.int32).reshape(a)
    onehot = (eid[:, None] == jnp.arange(n_exp_total, dtype=jnp.int32)[None, :]).astype(jnp.int32)
    csum = jnp.cumsum(onehot, axis=0)
    rank = jnp.sum(csum * onehot, axis=1) - 1
    counts = csum[-1]
    padded = ((counts + EXPERT_TILE - 1) // EXPERT_TILE) * EXPERT_TILE
    ends = jnp.cumsum(padded)
    starts = ends - padded
    dest = starts[eid] + rank
    nt = -(-a // EXPERT_TILE) + n_exp_total
    p = nt * EXPERT_TILE
    src_tok = jnp.zeros((p,), jnp.int32).at[dest].set(jnp.arange(a, dtype=jnp.int32) // TOP_K)
    tile_start = jnp.arange(nt, dtype=jnp.int32) * EXPERT_TILE
    tile_expert = jnp.minimum(jnp.sum(tile_start[:, None] >= ends[None, :], axis=1), n_exp_total - 1).astype(jnp.int32)
    n_used = (ends[-1] // EXPERT_TILE).astype(jnp.int32).reshape(1)
    return dest.astype(jnp.int32), src_tok, tile_expert, n_used


def kernel(x_prompt, x_sample, state_hgrn, state_conv, state_gla, cache_mem_k, cache_mem_v, mem_prompt, w_in_a, lb_a, conv_w, gnorm_a, w_out_a, w_in_c, w_gk2, b_gk2, gnorm_c, w_out_c, w_mq, w_mk, w_mv, w_mo, ln_g, ln_b, w_grp, b_grp, w_er, b_er, w_gate, w_up, w_down):
    bsz, seq, d = x_prompt.shape
    dec_b, dec_t, _ = x_sample.shape
    depth = w_mq.shape[0]
    _, _, ha_h, ha_dk, ha_dv = state_hgrn.shape
    _, _, gl_h, gl_dk, gl_dv = state_gla.shape
    _, _, mlen, mem_h, mem_hd = cache_mem_k.shape
    n_groups, n_exp = w_er.shape[1], w_er.shape[3]
    sc_w = conv_w.shape[1]
    sc_k = conv_w.shape[2]
    ha_f, ha_w = ha_h * ha_dk, ha_h * ha_dv
    gl_key, gl_val = gl_h * gl_dk, gl_h * gl_dv
    rank = w_gk2.shape[1]
    alpha = (2.0 * depth) ** 0.25
    assert dec_t <= SAMPLE_ROWS and sc_k == 3 and seq % CHUNK == 0 and rank <= LANES
    assert n_groups + n_groups * n_exp <= LANES

    rows_p = bsz * seq
    rows_s = dec_b * SAMPLE_ROWS
    xs_pad = jnp.pad(x_sample, ((0, 0), (0, SAMPLE_ROWS - dec_t), (0, 0)))
    x = jnp.concatenate([x_prompt.reshape(rows_p, d), xs_pad.reshape(rows_s, d)], axis=0)
    xb = x.astype(BF16)
    consts = _rec_constants()
    sample_sb = 4 if dec_b % 4 == 0 else 1

    mem_b = mem_prompt.astype(BF16)
    mk_p = _mem_kv(mem_b, w_mk.astype(BF16), mem_h, "mem_k_proj")
    mv_p = _mem_kv(mem_b, w_mv.astype(BF16), mem_h, "mem_v_proj")
    ck = cache_mem_k.reshape(depth * dec_b, mlen, mem_h, mem_hd)
    cv = cache_mem_v.reshape(depth * dec_b, mlen, mem_h, mem_hd)
    st_hgrn = state_hgrn.reshape(-1, ha_h, ha_dk, ha_dv)
    st_gla = state_gla.reshape(-1, gl_h, gl_dk, gl_dv)

    lb = jnp.cumsum(jax.nn.softmax(lb_a.astype(F32), axis=0), axis=0)
    lb = lb - lb[0]
    log_lb, log1m_lb, onem_lb = jnp.log(lb), jnp.log1p(-lb), 1.0 - lb

    z_hgrn = jnp.zeros((bsz, ha_h, ha_dk, ha_dv), F32)
    z_conv = jnp.zeros((bsz, sc_k - 1, sc_w), F32)
    z_gla = jnp.zeros((bsz, gl_h, gl_dk, gl_dv), F32)

    hgrn_p, conv_p, gla_p, hgrn_s, conv_s, gla_s = [], [], [], [], [], []
    for l in range(depth):
        j = l // 2
        if l % 2 == 0:
            y = _matmul(xb, w_in_a[j].astype(BF16), name=f"in_proj_{l}")
            extras = (log_lb[j:j + 1], log1m_lb[j:j + 1], onem_lb[j:j + 1])
            cols = dict(mode="hgrn", col_q=0, col_k=ha_f, col_v=2 * ha_f, col_g=2 * ha_f + ha_w)
            o_p, s_p = _recurrence(y, extras, gnorm_a[j], z_hgrn, consts, row0=0, nseq=bsz, tb=seq,
                                   t_valid=seq, hb=ha_h, tblk=_pick_tile(seq, 512), name=f"hgrn_prompt_{l}", **cols)
            o_s, s_s = _recurrence(y, extras, gnorm_a[j], st_hgrn, consts, row0=rows_p, nseq=dec_b, s_off=j * dec_b,
                                   tb=SAMPLE_ROWS, t_valid=dec_t, hb=ha_h, sb=sample_sb, name=f"hgrn_sample_{l}", **cols)
            c0 = 2 * ha_f + 2 * ha_w
            w3 = conv_w[j].T
            ccols = dict(col_b=c0, col_c=c0 + sc_w, col_h=c0 + 2 * sc_w, width=sc_w)
            c_p, t_p = _short_conv(y, w3, z_conv, row0=0, nseq=bsz, tb=seq, name=f"conv_prompt_{l}", **ccols)
            c_s, t_s = _short_conv(y, w3, state_conv[j], row0=rows_p, nseq=dec_b, tb=SAMPLE_ROWS,
                                   name=f"conv_sample_{l}", **ccols)
            hgrn_p.append(s_p)
            hgrn_s.append(s_s)
            conv_p.append(t_p[:, SUBLANES - (sc_k - 1):])
            conv_s.append(t_s[:, dec_t - (sc_k - 1):dec_t])
            acts = [(o_p, o_s), (c_p, c_s)]
            wo = w_out_a[j].astype(BF16)
            weights = [wo[:ha_w], wo[ha_w:]]
        else:
            ncol = 2 * gl_key + 2 * gl_val
            w_in = jnp.pad(w_in_c[j], ((0, 0), (0, LANES - rank))).astype(BF16)
            y = _matmul(xb, w_in, name=f"in_proj_{l}")
            wg2 = jnp.pad(w_gk2[j], ((0, LANES - rank), (0, 0)))
            wgh = wg2.astype(BF16)
            wgl = (wg2 - wgh.astype(F32)).astype(BF16)
            extras = (wgh, wgl, b_gk2[j].reshape(1, gl_key))
            cols = dict(mode="gla", col_q=0, col_k=gl_key, col_v=2 * gl_key, col_g=2 * gl_key + gl_val, col_gz=ncol)
            o_p, s_p = _recurrence(y, extras, gnorm_c[j], z_gla, consts, row0=0, nseq=bsz, tb=seq,
                                   t_valid=seq, hb=gl_h, tblk=_pick_tile(seq, 512), name=f"gla_prompt_{l}", **cols)
            o_s, s_s = _recurrence(y, extras, gnorm_c[j], st_gla, consts, row0=rows_p, nseq=dec_b, s_off=j * dec_b,
                                   tb=SAMPLE_ROWS, t_valid=dec_t, hb=gl_h, sb=sample_sb, name=f"gla_sample_{l}", **cols)
            gla_p.append(s_p)
            gla_s.append(s_s)
            acts = [(o_p, o_s)]
            weights = [w_out_c[j].astype(BF16)]
        x, xb = _proj_ln(acts, weights, x, ln_g[l, 0], ln_b[l, 0], alpha=alpha, name=f"mix_out_ln_{l}")

        q = _matmul(xb, w_mq[l].astype(BF16), name=f"mem_q_{l}")
        a_p = _mem_attention(q, mk_p, mv_p, kv0=l * bsz, row0=0, nseq=bsz, tb=seq, nb=1, name=f"mem_attn_prompt_{l}")
        a_s = _mem_attention(q, ck, cv, kv0=l * dec_b, row0=rows_p, nseq=dec_b, tb=SAMPLE_ROWS,
                             nb=4 if dec_b % 4 == 0 else 1, name=f"mem_attn_sample_{l}")
        wr = jnp.concatenate([w_grp[l]] + [w_er[l, gi] for gi in range(n_groups)], axis=1)
        wr = jnp.pad(wr, ((0, 0), (0, LANES - wr.shape[1])))
        wrh = wr.astype(BF16)
        wrl = (wr - wrh.astype(F32)).astype(BF16)
        br = jnp.concatenate([b_grp[l], b_er[l].reshape(-1)])
        br = jnp.pad(br, (0, LANES - br.shape[0])).reshape(1, LANES)
        x, xb, rinfo, xt = _proj_ln([(a_p, a_s)], [w_mo[l].astype(BF16)], x, ln_g[l, 1],
                                    ln_b[l, 1], alpha=alpha, router_w=(wrh, wrl, br), router=(n_groups, n_exp),
                                    name=f"mem_out_ln_router_{l}")

        ne = n_groups * n_exp
        nsub = d // LANES
        dest, src_tok, tile_expert, n_used = _dispatch_plan(rinfo, ne)
        f = w_gate.shape[-1]
        ys = _experts(xt, src_tok, tile_expert + l * ne, n_used, w_gate.reshape(depth * ne, d, f),
                      w_up.reshape(depth * ne, d, f), w_down.reshape(depth * ne, f, d))
        x, xb = _combine_ln(ys, dest[0::TOP_K], dest[1::TOP_K], rinfo, x, ln_g[l, 2], ln_b[l, 2],
                            alpha=alpha, name=f"moe_combine_ln_{l}")

    y_prompt = x[:rows_p].reshape(bsz, seq, d)
    y_sample = x[rows_p:].reshape(dec_b, SAMPLE_ROWS, d)[:, :dec_t]
    return (y_prompt, y_sample, jnp.stack(hgrn_p), jnp.stack(conv_p), jnp.stack(gla_p),
            mk_p.reshape(depth, bsz, mlen, mem_h, mem_hd), mv_p.reshape(depth, bsz, mlen, mem_h, mem_hd),
            jnp.stack(hgrn_s), jnp.stack(conv_s), jnp.stack(gla_s))
```

```python
import functools
import math

import numpy as np
import jax
import jax.numpy as jnp
from jax import lax
from jax.experimental import pallas as pl
from jax.experimental.pallas import tpu as pltpu

F32 = jnp.float32
BF16 = jnp.bfloat16

LN_EPS = 1e-5
RMS_EPS = 1e-6
GLA_NORMALIZER = 16.0
TOP_K = 2

LANES = 128
SUBLANES = 8
CHUNK = 128
SAMPLE_ROWS = 8
EXPERT_TILE = 256
EXPERT_LOOKAHEAD = 2
VMEM_LIMIT = 48 * 1024 * 1024

_NT = (((1,), (1,)), ((), ()))
_TN = (((0,), (0,)), ((), ()))


def _params(*sem):
    return pltpu.CompilerParams(dimension_semantics=sem, vmem_limit_bytes=VMEM_LIMIT)


def _pick_tile(n, pref):
    t = pref
    while t > SUBLANES and n % t:
        t //= 2
    assert n % t == 0, (n, pref)
    return t


def _dot(a, b):
    return jnp.dot(a, b, preferred_element_type=F32)


def _sigmoid(x):
    return 1.0 / (1.0 + jnp.exp(-x))


def _log_sigmoid(x):
    return jnp.minimum(x, 0.0) - jnp.log1p(jnp.exp(-jnp.abs(x)))


def _split2(a):
    hi = a.astype(BF16)
    lo = (a - hi.astype(F32)).astype(BF16)
    return hi, lo


def _split3(a):
    hi = a.astype(BF16)
    r = a - hi.astype(F32)
    mid = r.astype(BF16)
    lo = (r - mid.astype(F32)).astype(BF16)
    return hi, mid, lo


def _pad_rows(x, n):
    if x.shape[0] == n:
        return x
    return jnp.concatenate([x, jnp.zeros((n - x.shape[0], x.shape[1]), x.dtype)], axis=0)


def _store_token_tiles(ref, y):
    rows, d = y.shape
    nsub = d // LANES
    for s in range(nsub):
        ref[pl.ds(s, rows, stride=nsub), :] = y[:, s * LANES:(s + 1) * LANES]


def _load_token_tiles(ref, rows, d, row0=0):
    nsub = d // LANES
    return jnp.concatenate([ref[pl.ds(row0 * nsub + s, rows, stride=nsub), :] for s in range(nsub)], axis=1)


def _mm_kernel(x_ref, w_ref, o_ref):
    o_ref[...] = _dot(x_ref[...], w_ref[...]).astype(o_ref.dtype)


def _col_tile(m, pref=1024):
    return max(t for t in range(LANES, pref + 1, LANES) if m % t == 0)


def _matmul(x, w, *, out_dtype=F32, name):
    n, k = x.shape
    m = w.shape[1]
    tm = _pick_tile(n, 1024)
    tn = _col_tile(m)
    return pl.pallas_call(
        _mm_kernel,
        grid=(n // tm, m // tn),
        in_specs=[pl.BlockSpec((tm, k), lambda i, j: (i, 0)),
                  pl.BlockSpec((k, tn), lambda i, j: (0, j))],
        out_specs=pl.BlockSpec((tm, tn), lambda i, j: (i, j)),
        out_shape=jax.ShapeDtypeStruct((n, m), out_dtype),
        compiler_params=_params("parallel", "parallel"),
        name=name,
    )(x, w)


def _mem_kv_kernel(x_ref, w_ref, o_ref, *, heads):
    y = _dot(x_ref[0], w_ref[0])
    hd = y.shape[1] // heads
    for hh in range(heads):
        o_ref[0, :, hh, :] = y[:, hh * hd:(hh + 1) * hd]


def _mem_kv(mem_bf16, w_all, heads, name):
    nb, mlen, d = mem_bf16.shape
    nj, _, dout = w_all.shape
    return pl.pallas_call(
        functools.partial(_mem_kv_kernel, heads=heads),
        grid=(nj, nb),
        in_specs=[pl.BlockSpec((1, mlen, d), lambda j, i: (i, 0, 0)),
                  pl.BlockSpec((1, d, dout), lambda j, i: (j, 0, 0))],
        out_specs=pl.BlockSpec((1, mlen, heads, dout // heads), lambda j, i: (j * nb + i, 0, 0, 0)),
        out_shape=jax.ShapeDtypeStruct((nj * nb, mlen, heads, dout // heads), F32),
        compiler_params=_params("parallel", "parallel"),
        name=name,
    )(mem_bf16, w_all)


def _layer_norm_rows(z, g, b):
    mu = jnp.mean(z, axis=-1, keepdims=True)
    zc = z - mu
    var = jnp.mean(zc * zc, axis=-1, keepdims=True)
    return zc * lax.rsqrt(var + LN_EPS) * g + b


def _route(logits, n_groups, n_exp):
    lane = lax.broadcasted_iota(jnp.int32, logits.shape, 1).astype(F32)
    big = jnp.float32(1 << 20)
    neg = jnp.float32(-jnp.inf)
    gl = jnp.where(lane < n_groups, logits, neg)
    gmax = jnp.max(gl, axis=-1, keepdims=True)
    gsel = jnp.min(jnp.where(gl == gmax, lane, big), axis=-1, keepdims=True)
    gden = jnp.sum(jnp.exp(gl - gmax), axis=-1, keepdims=True)
    g_w = 1.0 / gden
    lo = n_groups + gsel * n_exp
    el = jnp.where((lane >= lo) & (lane < lo + n_exp), logits, neg)
    m1 = jnp.max(el, axis=-1, keepdims=True)
    i1 = jnp.min(jnp.where(el == m1, lane, big), axis=-1, keepdims=True)
    el2 = jnp.where(lane == i1, neg, el)
    m2 = jnp.max(el2, axis=-1, keepdims=True)
    i2 = jnp.min(jnp.where(el2 == m2, lane, big), axis=-1, keepdims=True)
    t = jnp.exp(m2 - m1)
    w1 = g_w / (1.0 + t)
    w2 = g_w * t / (1.0 + t)
    e1 = i1 - n_groups
    e2 = i2 - n_groups
    zero = jnp.zeros(logits.shape, F32)
    return jnp.where(lane == 0, w1, jnp.where(lane == 1, w2, jnp.where(lane == 2, e1, jnp.where(lane == 3, e2, zero))))


def _proj_ln_kernel(*refs, n_in, alpha, router, prompt_tiles):
    ap_refs = refs[:n_in]
    as_refs = refs[n_in:2 * n_in]
    w_refs = refs[2 * n_in:3 * n_in]
    x_ref, g_ref, b_ref = refs[3 * n_in:3 * n_in + 3]
    pos = 3 * n_in + 3
    if router is not None:
        wrh_ref, wrl_ref, br_ref = refs[pos:pos + 3]
        pos += 3
    xf_ref, xb_ref = refs[pos:pos + 2]
    is_prompt = pl.program_id(0) < prompt_tiles
    acc = None
    for ap_ref, as_ref, w_ref in zip(ap_refs, as_refs, w_refs):
        part = _dot(jnp.where(is_prompt, ap_ref[...], as_ref[...]), w_ref[...])
        acc = part if acc is None else acc + part
    y = _layer_norm_rows(alpha * x_ref[...] + acc, g_ref[...], b_ref[...])
    xf_ref[...] = y
    xb_ref[...] = y.astype(BF16)
    if router is not None:
        hi, lo = _split2(y)
        logits = _dot(hi, wrh_ref[...]) + _dot(lo, wrh_ref[...]) + _dot(hi, wrl_ref[...]) + br_ref[...]
        refs[pos + 2][...] = _route(logits, *router)
        _store_token_tiles(refs[pos + 3], y)


def _proj_ln(acts, weights, x, g, b, *, alpha, router_w=None, router=None, name):
    n, d = x.shape
    n_p = acts[0][0].shape[0]
    tm = _pick_tile(math.gcd(n_p, n - n_p), 512)
    npt = n_p // tm
    row = lambda i: (i, 0)
    fix = lambda i: (0, 0)
    in_specs = [pl.BlockSpec((tm, a[0].shape[1]), lambda i: (jnp.minimum(i, npt - 1), 0)) for a in acts]
    in_specs += [pl.BlockSpec((tm, a[1].shape[1]), lambda i: (jnp.maximum(i - npt, 0), 0)) for a in acts]
    in_specs += [pl.BlockSpec(w.shape, fix) for w in weights]
    in_specs += [pl.BlockSpec((tm, d), row), pl.BlockSpec((1, d), fix), pl.BlockSpec((1, d), fix)]
    args = [a[0] for a in acts] + [a[1] for a in acts] + list(weights) + [x, g.reshape(1, d), b.reshape(1, d)]
    out_specs = [pl.BlockSpec((tm, d), row), pl.BlockSpec((tm, d), row)]
    out_shape = [jax.ShapeDtypeStruct((n, d), F32), jax.ShapeDtypeStruct((n, d), BF16)]
    if router_w is not None:
        in_specs += [pl.BlockSpec(r.shape, fix) for r in router_w]
        args += list(router_w)
        out_specs.append(pl.BlockSpec((tm, LANES), row))
        out_shape.append(jax.ShapeDtypeStruct((n, LANES), F32))
        out_specs.append(pl.BlockSpec((tm * (d // LANES), LANES), row))
        out_shape.append(jax.ShapeDtypeStruct((n * (d // LANES), LANES), F32))
    return pl.pallas_call(
        functools.partial(_proj_ln_kernel, n_in=len(acts), alpha=alpha, router=router, prompt_tiles=npt),
        grid=(n // tm,),
        in_specs=in_specs, out_specs=out_specs, out_shape=out_shape,
        compiler_params=_params("parallel"),
        name=name,
    )(*args)


_LEVELS = (8, 16, 32, 64)


def _rec_constants():
    t = np.arange(CHUNK)
    tri = (t[:, None] >= t[None, :]).astype(np.float32)
    masks = []
    for lvl in _LEVELS:
        bt, bs = t[:, None] // lvl, t[None, :] // lvl
        masks.append(((bt - bs == 1) & (bt % 2 == 1)).astype(np.float32))
    return jnp.asarray(tri, BF16), jnp.asarray(np.stack(masks), F32)


def _rec_chunk(q, k, a, v, st, tri_ref, mask_ref, nv):
    r = SUBLANES * nv
    m = max(r, 2 * SUBLANES)
    dk = q.shape[1]

    if nv == 1:
        row_k = lax.broadcasted_iota(jnp.int32, (SUBLANES, dk), 0)
        row_v = lax.broadcasted_iota(jnp.int32, v.shape, 0)
        b = a
        for sh in (1, 2, 4):
            b = b + jnp.where(row_k >= sh, pltpu.roll(b, sh, axis=0), 0.0)
        b_last = b[r - 1:r]
        o = jnp.zeros(v.shape, F32)
        for s_ in range(SUBLANES):
            e = jnp.exp(jnp.minimum(b - b[s_:s_ + 1], 0.0))
            col = jnp.sum(q * k[s_:s_ + 1] * e, axis=1, keepdims=True)
            o = o + jnp.where(row_v >= s_, col, 0.0) * v[s_:s_ + 1]
        qe = _pad_rows(q * jnp.exp(b), m).astype(BF16)
        o = o + lax.dot_general(qe, st.astype(BF16), _NT, preferred_element_type=F32)[0:r]
        kd = _pad_rows(k * jnp.exp(b_last - b), CHUNK).astype(BF16)
        v_bf = _pad_rows(v, CHUNK).astype(BF16)
        st_new = st * jnp.exp(b_last) + lax.dot_general(v_bf, kd, _TN, preferred_element_type=F32)
        return o, st_new

    hi, mid, lo = _split3(_pad_rows(a, CHUNK))
    tri = tri_ref[0:m, :]
    b = (_dot(tri, hi) + _dot(tri, mid) + _dot(tri, lo))[0:r]
    b_last = b[r - 1:r]

    scores = None
    for li, lvl in enumerate(_LEVELS):
        if lvl >= r:
            continue
        qp, kp = [], []
        for blk in range(r // lvl):
            rows = slice(blk * lvl, (blk + 1) * lvl)
            if blk % 2 == 1:
                rho = b[blk * lvl - 1:blk * lvl]
                qp.append(q[rows] * jnp.exp(b[rows] - rho))
                kp.append(jnp.zeros((lvl, dk), F32))
            else:
                rho = b[(blk + 1) * lvl - 1:(blk + 1) * lvl]
                kp.append(k[rows] * jnp.exp(rho - b[rows]))
                qp.append(jnp.zeros((lvl, dk), F32))
        qh = jnp.concatenate(qp, axis=0).astype(BF16)
        kh = jnp.concatenate(kp, axis=0).astype(BF16)
        s = lax.dot_general(qh, kh, _NT, preferred_element_type=F32) * mask_ref[li]
        scores = s if scores is None else scores + s

    lane = lax.broadcasted_iota(jnp.int32, (SUBLANES, CHUNK), 1)
    sub = lax.broadcasted_iota(jnp.int32, (SUBLANES, CHUNK), 0)
    dparts = []
    for j in range(nv):
        rows = slice(SUBLANES * j, SUBLANES * (j + 1))
        qj, kj, bj = q[rows], k[rows], b[rows]
        acc = jnp.zeros((SUBLANES, CHUNK), F32)
        for s_ in range(SUBLANES):
            e = jnp.exp(jnp.minimum(bj - bj[s_:s_ + 1], 0.0))
            col = jnp.sum(qj * kj[s_:s_ + 1] * e, axis=1, keepdims=True)
            acc = jnp.where(lane == SUBLANES * j + s_, col, acc)
        dparts.append(jnp.where(lane <= sub + SUBLANES * j, acc, 0.0))
    diag = _pad_rows(jnp.concatenate(dparts, axis=0) if nv > 1 else dparts[0], m)
    scores = diag if scores is None else scores + diag

    v_bf = _pad_rows(v, CHUNK).astype(BF16)
    o = _dot(scores.astype(BF16), v_bf)
    qe = _pad_rows(q * jnp.exp(b), m).astype(BF16)
    o = o + lax.dot_general(qe, st.astype(BF16), _NT, preferred_element_type=F32)
    kd = _pad_rows(k * jnp.exp(b_last - b), CHUNK).astype(BF16)
    st_new = st * jnp.exp(b_last) + lax.dot_general(v_bf, kd, _TN, preferred_element_type=F32)
    return o[0:r], st_new


def _rec_kernel(*refs, mode, hb, sb, dk, dv, nv, nchunks, t_valid, qscale):
    q_ref, k_ref, v_ref, g_ref = refs[:4]
    pos = 4
    if mode == "gla":
        gz_ref, wgh_ref, wgl_ref, bg_ref = refs[pos:pos + 4]
        pos += 4
    else:
        loglb_ref, log1m_ref, onem_ref = refs[pos:pos + 3]
        pos += 3
    nw_ref, s0_ref, tri_ref, mask_ref = refs[pos:pos + 4]
    o_ref, so_ref, st_scr = refs[pos + 4:pos + 7]
    r = SUBLANES * nv
    tstep = pl.program_id(2)

    @pl.when(tstep == 0)
    def _():
        for sq in range(sb):
            for hh in range(hb):
                st_scr[sq * hb + hh] = s0_ref[sq, hh].T

    def chunk_step(c, carry, sq=0):
        rsl = pl.ds(c * r if isinstance(c, int) else pl.multiple_of(c * r, r), r)
        if t_valid < r:
            valid_k = lax.broadcasted_iota(jnp.int32, (r, dk), 0) < t_valid
            valid_v = lax.broadcasted_iota(jnp.int32, (r, dv), 0) < t_valid
        if mode == "gla":
            gz = _pad_rows(gz_ref[rsl, :], max(r, 2 * SUBLANES))
            gzh, gzl = _split2(gz)
        for hh in range(hb):
            ksl = slice(hh * dk, (hh + 1) * dk)
            vsl = slice(hh * dv, (hh + 1) * dv)
            qr = q_ref[rsl, ksl]
            kr = k_ref[rsl, ksl]
            v = v_ref[rsl, vsl]
            if mode == "gla":
                gk = (_dot(gzh, wgh_ref[:, ksl]) + _dot(gzl, wgh_ref[:, ksl]) + _dot(gzh, wgl_ref[:, ksl]))[0:r]
                a = _log_sigmoid(gk + bg_ref[:, ksl]) / GLA_NORMALIZER
                q = qr * qscale
                k = kr
            else:
                ls = _log_sigmoid(kr)
                x2 = log1m_ref[:, ksl] + ls
                x1 = loglb_ref[:, ksl]
                a = jnp.maximum(x1, x2) + jnp.log1p(jnp.exp(-jnp.abs(x1 - x2)))
                k = onem_ref[:, ksl] * _sigmoid(-kr)
                q = qr * _sigmoid(qr)
            if t_valid < r:
                a = jnp.where(valid_k, a, 0.0)
                k = jnp.where(valid_k, k, 0.0)
                v = jnp.where(valid_v, v, 0.0)
            o, st_new = _rec_chunk(q, k, a, v, st_scr[sq * hb + hh], tri_ref, mask_ref, nv)
            st_scr[sq * hb + hh] = st_new
            g = g_ref[rsl, vsl]
            y = o * lax.rsqrt(jnp.mean(o * o, axis=-1, keepdims=True) + RMS_EPS) * nw_ref[...]
            o_ref[rsl, vsl] = (y * (g * _sigmoid(g))).astype(o_ref.dtype)
        return carry

    if nchunks == 1:
        for sq in range(sb):
            chunk_step(sq, 0, sq)
    else:
        lax.fori_loop(0, nchunks, chunk_step, 0, unroll=2)

    @pl.when(tstep == pl.num_programs(2) - 1)
    def _():
        for sq in range(sb):
            for hh in range(hb):
                so_ref[sq, hh] = st_scr[sq * hb + hh].T


def _recurrence(y, extras, norm_w, s0, consts, *, mode, col_q, col_k, col_v, col_g, col_gz=None,
                row0, nseq, tb, t_valid, hb, name, s_off=0, tblk=None, sb=1):
    _, h, dk, dv = s0.shape
    tblk = tb if tblk is None else tblk
    nt = tb // tblk
    nv = min(tblk, CHUNK) // SUBLANES
    nchunks = tblk // (SUBLANES * nv)
    assert h % hb == 0 and tb % tblk == 0 and nseq % sb == 0 and s_off % sb == 0
    assert sb == 1 or (nt == 1 and nchunks == 1)
    rblk = sb * tblk
    assert row0 % rblk == 0
    rb0 = row0 // rblk
    sblk0 = s_off // sb
    kw, vw = hb * dk, hb * dv
    tri, masks = consts

    def cspec(width, col):
        assert col % width == 0
        cb = col // width
        return pl.BlockSpec((rblk, width), lambda i, j, t: (rb0 + i * nt + t, cb + j))

    fixh = lambda width: pl.BlockSpec((1, width), lambda i, j, t: (0, j))
    in_specs = [cspec(kw, col_q), cspec(kw, col_k), cspec(vw, col_v), cspec(vw, col_g)]
    args = [y, y, y, y]
    if mode == "gla":
        wgh, wgl, bg = extras
        in_specs += [pl.BlockSpec((rblk, LANES), lambda i, j, t: (rb0 + i * nt + t, col_gz // LANES)),
                     pl.BlockSpec((LANES, kw), lambda i, j, t: (0, j)),
                     pl.BlockSpec((LANES, kw), lambda i, j, t: (0, j)), fixh(kw)]
        args += [y, wgh, wgl, bg]
    else:
        in_specs += [fixh(kw), fixh(kw), fixh(kw)]
        args += list(extras)
    in_specs += [pl.BlockSpec((1, dv), lambda i, j, t: (0, 0)),
                 pl.BlockSpec((sb, hb, dk, dv), lambda i, j, t: (sblk0 + i, j, 0, 0)),
                 pl.BlockSpec(tri.shape, lambda i, j, t: (0, 0)),
                 pl.BlockSpec(masks.shape, lambda i, j, t: (0, 0, 0))]
    args += [norm_w.reshape(1, dv), s0, tri, masks]
    return pl.pallas_call(
        functools.partial(_rec_kernel, mode=mode, hb=hb, sb=sb, dk=dk, dv=dv, nv=nv, nchunks=nchunks,
                          t_valid=t_valid, qscale=dk ** -0.5),
        grid=(nseq // sb, h // hb, nt),
        in_specs=in_specs,
        out_specs=[pl.BlockSpec((rblk, vw), lambda i, j, t: (i * nt + t, j)),
                   pl.BlockSpec((sb, hb, dk, dv), lambda i, j, t: (i, j, 0, 0))],
        out_shape=[jax.ShapeDtypeStruct((nseq * tb, h * dv), BF16),
                   jax.ShapeDtypeStruct((nseq, h, dk, dv), F32)],
        scratch_shapes=[pltpu.VMEM((sb * hb, dv, dk), F32)],
        compiler_params=_params("parallel", "parallel", "arbitrary"),
        name=name,
    )(*args)


def _conv_kernel(bg_ref, cg_ref, h_ref, w_ref, buf_ref, y_ref, tail_ref, carry):
    t = pl.program_id(1)
    u = cg_ref[...] * h_ref[...]
    tt = u.shape[0]

    @pl.when(t == 0)
    def _():
        carry[...] = jnp.zeros(carry.shape, F32)
        carry[SUBLANES - 2:SUBLANES, :] = buf_ref[0]

    p6 = carry[SUBLANES - 2:SUBLANES - 1, :]
    p7 = carry[SUBLANES - 1:SUBLANES, :]
    row = lax.broadcasted_iota(jnp.int32, u.shape, 0)
    u1 = jnp.where(row == 0, p7, pltpu.roll(u, 1, axis=0))
    u2 = jnp.where(row == 0, p6, jnp.where(row == 1, p7, pltpu.roll(u, 2, axis=0)))
    y = bg_ref[...] * (u2 * w_ref[0:1, :] + u1 * w_ref[1:2, :] + u * w_ref[2:3, :])
    y_ref[...] = y.astype(y_ref.dtype)
    tail = u[tt - SUBLANES:tt, :]
    carry[...] = tail
    tail_ref[0] = tail


def _short_conv(y, w3, buf, *, col_b, col_c, col_h, width, row0, nseq, tb, name):
    tt = _pick_tile(tb, 512)
    nt = tb // tt
    rb0 = row0 // tt

    def cspec(col):
        cb = col // width
        return pl.BlockSpec((tt, width), lambda i, t: (rb0 + i * nt + t, cb))

    return pl.pallas_call(
        _conv_kernel,
        grid=(nseq, nt),
        in_specs=[cspec(col_b), cspec(col_c), cspec(col_h),
                  pl.BlockSpec((3, width), lambda i, t: (0, 0)),
                  pl.BlockSpec((1, 2, width), lambda i, t: (i, 0, 0))],
        out_specs=[pl.BlockSpec((tt, width), lambda i, t: (i * nt + t, 0)),
                   pl.BlockSpec((1, SUBLANES, width), lambda i, t: (i, 0, 0))],
        out_shape=[jax.ShapeDtypeStruct((nseq * tb, width), BF16),
                   jax.ShapeDtypeStruct((nseq, SUBLANES, width), F32)],
        scratch_shapes=[pltpu.VMEM((SUBLANES, width), F32)],
        compiler_params=_params("parallel", "arbitrary"),
        name=name,
    )(y, y, y, w3, buf)


def _attn_kernel(q_ref, k_ref, v_ref, o_ref, kscr, vscr, *, nb, rows, heads, hd):
    scale = hd ** -0.5
    m = max(rows, 2 * SUBLANES)
    mlen = k_ref.shape[1]
    for bi in range(nb):
        k2 = k_ref[bi].reshape(mlen * heads, hd)
        v2 = v_ref[bi].reshape(mlen * heads, hd)
        for c in range(hd // LANES):
            kscr[bi, c] = k2[:, c * LANES:(c + 1) * LANES]
            vscr[bi, c] = v2[:, c * LANES:(c + 1) * LANES]
        qb = _pad_rows(q_ref[bi * rows:(bi + 1) * rows, :], m)
        outs = []
        for hh in range(heads):
            sl = slice(hh * hd, (hh + 1) * hd)
            head_rows = pl.ds(hh, mlen, stride=heads)
            kh = jnp.concatenate([kscr[bi, c, head_rows, :] for c in range(hd // LANES)], axis=1).astype(BF16)
            vh = jnp.concatenate([vscr[bi, c, head_rows, :] for c in range(hd // LANES)], axis=1).astype(BF16)
            s = lax.dot_general(qb[:, sl].astype(BF16), kh, _NT, preferred_element_type=F32) * scale
            p = jnp.exp(s - jnp.max(s, axis=-1, keepdims=True))
            p = p / jnp.sum(p, axis=-1, keepdims=True)
            outs.append(_dot(p.astype(BF16), vh))
        o = jnp.concatenate(outs, axis=-1)
        o_ref[bi * rows:(bi + 1) * rows, :] = o[0:rows].astype(o_ref.dtype)


def _attn_short_kernel(q_ref, k_ref, v_ref, o_ref, *, nb, rows, heads, hd):
    scale = hd ** -0.5
    mlen = k_ref.shape[1]
    nrow, ncol = heads * rows, mlen * heads
    col_head = lax.broadcasted_iota(jnp.int32, (nrow, ncol), 1) & (heads - 1)
    row_head = lax.shift_right_logical(lax.broadcasted_iota(jnp.int32, (nrow, ncol), 0),
                                       jnp.int32(rows.bit_length() - 1))
    own = col_head == row_head
    for bi in range(nb):
        k2 = k_ref[bi].reshape(ncol, hd).astype(BF16)
        v2 = v_ref[bi].reshape(ncol, hd).astype(BF16)
        qf = q_ref[bi * rows:(bi + 1) * rows, :]
        qs = jnp.concatenate([qf[:, hh * hd:(hh + 1) * hd] for hh in range(heads)], axis=0).astype(BF16)
        s = lax.dot_general(qs, k2, _NT, preferred_element_type=F32) * scale
        s = jnp.where(own, s, -jnp.inf)
        p = jnp.exp(s - jnp.max(s, axis=-1, keepdims=True))
        p = p / jnp.sum(p, axis=-1, keepdims=True)
        o2 = _dot(p.astype(BF16), v2)
        o = jnp.concatenate([o2[hh * rows:(hh + 1) * rows] for hh in range(heads)], axis=1)
        o_ref[bi * rows:(bi + 1) * rows, :] = o.astype(o_ref.dtype)


def _mem_attention(q, mem_k, mem_v, *, kv0, row0, nseq, tb, nb, name):
    d = q.shape[1]
    _, mlen, heads, hd = mem_k.shape
    assert kv0 % nb == 0
    kvb = kv0 // nb
    if nb > 1:
        rows, nt = tb, 1
    else:
        rows = _pick_tile(tb, 512)
        nt = tb // rows
    assert nseq % nb == 0 and row0 % (nb * rows) == 0
    rb0 = row0 // (nb * rows)
    short = nb > 1 and rows % SUBLANES == 0 and heads & (heads - 1) == 0 and rows & (rows - 1) == 0
    scratch = [] if short else [pltpu.VMEM((nb, hd // LANES, mlen * heads, LANES), F32)] * 2
    return pl.pallas_call(
        functools.partial(_attn_short_kernel if short else _attn_kernel, nb=nb, rows=rows, heads=heads, hd=hd),
        grid=(nseq // nb, nt),
        in_specs=[pl.BlockSpec((nb * rows, d), lambda i, t: (rb0 + i * nt + t, 0)),
                  pl.BlockSpec((nb, mlen, heads, hd), lambda i, t: (kvb + i, 0, 0, 0)),
                  pl.BlockSpec((nb, mlen, heads, hd), lambda i, t: (kvb + i, 0, 0, 0))],
        out_specs=pl.BlockSpec((nb * rows, d), lambda i, t: (i * nt + t, 0)),
        out_shape=jax.ShapeDtypeStruct((nseq * tb, d), BF16),
        scratch_shapes=scratch,
        compiler_params=_params("parallel", "parallel"),
        name=name,
    )(q, mem_k, mem_v)


def _tile_copy(src_hbm, tok, dst, r, sem, nsub):
    so = tok * nsub if isinstance(tok, int) else pl.multiple_of(tok * nsub, nsub)
    do = pl.multiple_of(r * nsub, nsub)
    return pltpu.make_async_copy(src_hbm.at[pl.ds(so, nsub)], dst.at[pl.ds(do, nsub)], sem)


def _gather_start(src_hbm, idx_ref, n, dst, sem, nsub):
    def body(g, c):
        for u in range(SUBLANES):
            r = g * SUBLANES + u
            _tile_copy(src_hbm, idx_ref[r], dst, r, sem, nsub).start(priority=u % 2)
        return c

    lax.fori_loop(0, n // SUBLANES, body, 0)


def _gather_wait(src_hbm, n, dst, sem, nsub):
    def body(r, c):
        _tile_copy(src_hbm, 0, dst, r, sem, nsub).wait()
        return c

    lax.fori_loop(0, n, body, 0, unroll=8)


def _expert_kernel(te_ref, nu_ref, *refs, d):
    depth = EXPERT_LOOKAHEAD
    idx_refs = refs[:depth + 1]
    x_hbm, wg_ref, wu_ref, wd_ref, o_ref, xbuf, wgb, wub, wdb, sem = refs[depth + 1:]
    i = pl.program_id(0)
    nu = nu_ref[0]
    nsub = d // LANES

    for k in range(depth):
        @pl.when((i == 0) & (k < nu))
        def _(k=k):
            _gather_start(x_hbm, idx_refs[k], EXPERT_TILE, xbuf.at[k], sem.at[k], nsub)

    @pl.when(i + depth < nu)
    def _():
        nxt = (i + depth) % (depth + 1)
        _gather_start(x_hbm, idx_refs[depth], EXPERT_TILE, xbuf.at[nxt], sem.at[nxt], nsub)

    @pl.when(i < nu)
    def _():
        @pl.when((i == 0) | (te_ref[i] != te_ref[jnp.maximum(i - 1, 0)]))
        def _():
            wgb[...] = wg_ref[0].astype(BF16)
            wub[...] = wu_ref[0].astype(BF16)
            wdb[...] = wd_ref[0].astype(BF16)

        slot = i % (depth + 1)
        _gather_wait(x_hbm, EXPERT_TILE, xbuf.at[slot], sem.at[slot], nsub)
        x = _load_token_tiles(xbuf.at[slot], EXPERT_TILE, d).astype(BF16)
        gate = _dot(x, wgb[...])
        up = _dot(x, wub[...])
        hid = (gate * _sigmoid(gate) * up).astype(BF16)
        _store_token_tiles(o_ref, _dot(hid, wdb[...]))

    @pl.when(i >= nu)
    def _():
        o_ref[...] = jnp.zeros(o_ref.shape, F32)


def _experts(xt, src_tok, tile_expert, n_used, wg, wu, wd):
    d, f = wg.shape[1], wg.shape[2]
    nsub = d // LANES
    p = src_tok.shape[0]
    nt = p // EXPERT_TILE
    live = lambda i, te, nu: jnp.minimum(i, nu[0] - 1)
    wspec = lambda shp: pl.BlockSpec(shp, lambda i, te, nu: (te[live(i, te, nu)], 0, 0))
    depth = EXPERT_LOOKAHEAD
    idx_spec = lambda k: pl.BlockSpec((EXPERT_TILE,), lambda i, te, nu: (jnp.minimum(i + k, nt - 1),),
                                      memory_space=pltpu.SMEM)
    grid_spec = pltpu.PrefetchScalarGridSpec(
        num_scalar_prefetch=2,
        grid=(nt,),
        in_specs=[idx_spec(k) for k in range(depth + 1)]
        + [pl.BlockSpec(memory_space=pl.ANY), wspec((1, d, f)), wspec((1, d, f)), wspec((1, f, d))],
        out_specs=pl.BlockSpec((EXPERT_TILE * nsub, LANES), lambda i, te, nu: (i, 0)),
        scratch_shapes=[pltpu.VMEM((depth + 1, EXPERT_TILE * nsub, LANES), F32),
                        pltpu.VMEM((d, f), BF16), pltpu.VMEM((d, f), BF16), pltpu.VMEM((f, d), BF16),
                        pltpu.SemaphoreType.DMA((depth + 1,))],
    )
    return pl.pallas_call(
        functools.partial(_expert_kernel, d=d),
        grid_spec=grid_spec,
        out_shape=jax.ShapeDtypeStruct((p * nsub, LANES), F32),
        compiler_params=_params("arbitrary"),
        name="moe_experts",
    )(tile_expert, n_used, *([src_tok] * (depth + 1)), xt, wg, wu, wd)


def _combine_ln_kernel(ia0_ref, ia1_ref, ib0_ref, ib1_ref, y_hbm, r_ref, x_ref, g_ref, b_ref, xf_ref, xb_ref,
                       ybuf, sem, *, alpha, nsteps):
    i = pl.program_id(0)
    rows, d = x_ref.shape
    nsub = d // LANES

    def start(idx_a, idx_b, slot):
        _gather_start(y_hbm, idx_a, rows, ybuf.at[slot, 0], sem.at[slot, 0], nsub)
        _gather_start(y_hbm, idx_b, rows, ybuf.at[slot, 1], sem.at[slot, 1], nsub)

    @pl.when(i == 0)
    def _():
        start(ia0_ref, ib0_ref, 0)

    @pl.when(i + 1 < nsteps)
    def _():
        start(ia1_ref, ib1_ref, (i + 1) % 2)

    slot = i % 2
    _gather_wait(y_hbm, rows, ybuf.at[slot, 0], sem.at[slot, 0], nsub)
    _gather_wait(y_hbm, rows, ybuf.at[slot, 1], sem.at[slot, 1], nsub)
    w0 = r_ref[:, 0:1]
    w1 = r_ref[:, 1:2]
    moe = w0 * _load_token_tiles(ybuf.at[slot, 0], rows, d) + w1 * _load_token_tiles(ybuf.at[slot, 1], rows, d)
    y = _layer_norm_rows(alpha * x_ref[...] + moe, g_ref[...], b_ref[...])
    xf_ref[...] = y
    xb_ref[...] = y.astype(BF16)


def _combine_ln(ys, dest_a, dest_b, rinfo, x, g, b, *, alpha, name):
    n, d = x.shape
    nsub = d // LANES
    tm = _pick_tile(n, 512)
    nsteps = n // tm
    row = lambda i: (i, 0)
    fix = lambda i: (0, 0)
    cur = pl.BlockSpec((tm,), lambda i: (i,), memory_space=pltpu.SMEM)
    nxt = pl.BlockSpec((tm,), lambda i: (jnp.minimum(i + 1, nsteps - 1),), memory_space=pltpu.SMEM)
    return pl.pallas_call(
        functools.partial(_combine_ln_kernel, alpha=alpha, nsteps=nsteps),
        grid=(nsteps,),
        in_specs=[cur, nxt, cur, nxt, pl.BlockSpec(memory_space=pl.ANY),
                  pl.BlockSpec((tm, LANES), row), pl.BlockSpec((tm, d), row),
                  pl.BlockSpec((1, d), fix), pl.BlockSpec((1, d), fix)],
        out_specs=[pl.BlockSpec((tm, d), row), pl.BlockSpec((tm, d), row)],
        out_shape=[jax.ShapeDtypeStruct((n, d), F32), jax.ShapeDtypeStruct((n, d), BF16)],
        scratch_shapes=[pltpu.VMEM((2, 2, tm * nsub, LANES), F32), pltpu.SemaphoreType.DMA((2, 2))],
        compiler_params=_params("arbitrary"),
        name=name,
    )(dest_a, dest_a, dest_b, dest_b, ys, rinfo, x, g.reshape(1, d), b.reshape(1, d))


def _dispatch_plan(rinfo, n_exp_total):
    n = rinfo.shape[0]
    a = n * TOP_K
    eid = rinfo[:, 2:2 + TOP_K].astype(jnp.int32).reshape(a)
    onehot = (eid[:, None] == jnp.arange(n_exp_total, dtype=jnp.int32)[None, :]).astype(jnp.int32)
    csum = jnp.cumsum(onehot, axis=0)
    rank = jnp.sum(csum * onehot, axis=1) - 1
    counts = csum[-1]
    padded = ((counts + EXPERT_TILE - 1) // EXPERT_TILE) * EXPERT_TILE
    ends = jnp.cumsum(padded)
    starts = ends - padded
    dest = starts[eid] + rank
    nt = -(-a // EXPERT_TILE) + n_exp_total
    p = nt * EXPERT_TILE
    src_tok = jnp.zeros((p,), jnp.int32).at[dest].set(jnp.arange(a, dtype=jnp.int32) // TOP_K)
    tile_start = jnp.arange(nt, dtype=jnp.int32) * EXPERT_TILE
    tile_expert = jnp.minimum(jnp.sum(tile_start[:, None] >= ends[None, :], axis=1), n_exp_total - 1).astype(jnp.int32)
    n_used = (ends[-1] // EXPERT_TILE).astype(jnp.int32).reshape(1)
    return dest.astype(jnp.int32), src_tok, tile_expert, n_used


def kernel(x_prompt, x_sample, state_hgrn, state_conv, state_gla, cache_mem_k, cache_mem_v, mem_prompt, w_in_a, lb_a, conv_w, gnorm_a, w_out_a, w_in_c, w_gk2, b_gk2, gnorm_c, w_out_c, w_mq, w_mk, w_mv, w_mo, ln_g, ln_b, w_grp, b_grp, w_er, b_er, w_gate, w_up, w_down):
    bsz, seq, d = x_prompt.shape
    dec_b, dec_t, _ = x_sample.shape
    depth = w_mq.shape[0]
    _, _, ha_h, ha_dk, ha_dv = state_hgrn.shape
    _, _, gl_h, gl_dk, gl_dv = state_gla.shape
    _, _, mlen, mem_h, mem_hd = cache_mem_k.shape
    n_groups, n_exp = w_er.shape[1], w_er.shape[3]
    sc_w = conv_w.shape[1]
    sc_k = conv_w.shape[2]
    ha_f, ha_w = ha_h * ha_dk, ha_h * ha_dv
    gl_key, gl_val = gl_h * gl_dk, gl_h * gl_dv
    rank = w_gk2.shape[1]
    alpha = (2.0 * depth) ** 0.25
    assert dec_t <= SAMPLE_ROWS and sc_k == 3 and seq % CHUNK == 0 and rank <= LANES
    assert n_groups + n_groups * n_exp <= LANES

    rows_p = bsz * seq
    rows_s = dec_b * SAMPLE_ROWS
    xs_pad = jnp.pad(x_sample, ((0, 0), (0, SAMPLE_ROWS - dec_t), (0, 0)))
    x = jnp.concatenate([x_prompt.reshape(rows_p, d), xs_pad.reshape(rows_s, d)], axis=0)
    xb = x.astype(BF16)
    consts = _rec_constants()
    sample_sb = 4 if dec_b % 4 == 0 else 1

    mem_b = mem_prompt.astype(BF16)
    mk_p = _mem_kv(mem_b, w_mk.astype(BF16), mem_h, "mem_k_proj")
    mv_p = _mem_kv(mem_b, w_mv.astype(BF16), mem_h, "mem_v_proj")
    ck = cache_mem_k.reshape(depth * dec_b, mlen, mem_h, mem_hd)
    cv = cache_mem_v.reshape(depth * dec_b, mlen, mem_h, mem_hd)
    st_hgrn = state_hgrn.reshape(-1, ha_h, ha_dk, ha_dv)
    st_gla = state_gla.reshape(-1, gl_h, gl_dk, gl_dv)

    lb = jnp.cumsum(jax.nn.softmax(lb_a.astype(F32), axis=0), axis=0)
    lb = lb - lb[0]
    log_lb, log1m_lb, onem_lb = jnp.log(lb), jnp.log1p(-lb), 1.0 - lb

    z_hgrn = jnp.zeros((bsz, ha_h, ha_dk, ha_dv), F32)
    z_conv = jnp.zeros((bsz, sc_k - 1, sc_w), F32)
    z_gla = jnp.zeros((bsz, gl_h, gl_dk, gl_dv), F32)

    hgrn_p, conv_p, gla_p, hgrn_s, conv_s, gla_s = [], [], [], [], [], []
    for l in range(depth):
        j = l // 2
        if l % 2 == 0:
            y = _matmul(xb, w_in_a[j].astype(BF16), name=f"in_proj_{l}")
            extras = (log_lb[j:j + 1], log1m_lb[j:j + 1], onem_lb[j:j + 1])
            cols = dict(mode="hgrn", col_q=0, col_k=ha_f, col_v=2 * ha_f, col_g=2 * ha_f + ha_w)
            o_p, s_p = _recurrence(y, extras, gnorm_a[j], z_hgrn, consts, row0=0, nseq=bsz, tb=seq,
                                   t_valid=seq, hb=ha_h, tblk=_pick_tile(seq, 512), name=f"hgrn_prompt_{l}", **cols)
            o_s, s_s = _recurrence(y, extras, gnorm_a[j], st_hgrn, consts, row0=rows_p, nseq=dec_b, s_off=j * dec_b,
                                   tb=SAMPLE_ROWS, t_valid=dec_t, hb=ha_h, sb=sample_sb, name=f"hgrn_sample_{l}", **cols)
            c0 = 2 * ha_f + 2 * ha_w
            w3 = conv_w[j].T
            ccols = dict(col_b=c0, col_c=c0 + sc_w, col_h=c0 + 2 * sc_w, width=sc_w)
            c_p, t_p = _short_conv(y, w3, z_conv, row0=0, nseq=bsz, tb=seq, name=f"conv_prompt_{l}", **ccols)
            c_s, t_s = _short_conv(y, w3, state_conv[j], row0=rows_p, nseq=dec_b, tb=SAMPLE_ROWS,
                                   name=f"conv_sample_{l}", **ccols)
            hgrn_p.append(s_p)
            hgrn_s.append(s_s)
            conv_p.append(t_p[:, SUBLANES - (sc_k - 1):])
            conv_s.append(t_s[:, dec_t - (sc_k - 1):dec_t])
            acts = [(o_p, o_s), (c_p, c_s)]
            wo = w_out_a[j].astype(BF16)
            weights = [wo[:ha_w], wo[ha_w:]]
        else:
            ncol = 2 * gl_key + 2 * gl_val
            w_in = jnp.pad(w_in_c[j], ((0, 0), (0, LANES - rank))).astype(BF16)
            y = _matmul(xb, w_in, name=f"in_proj_{l}")
            wg2 = jnp.pad(w_gk2[j], ((0, LANES - rank), (0, 0)))
            wgh = wg2.astype(BF16)
            wgl = (wg2 - wgh.astype(F32)).astype(BF16)
            extras = (wgh, wgl, b_gk2[j].reshape(1, gl_key))
            cols = dict(mode="gla", col_q=0, col_k=gl_key, col_v=2 * gl_key, col_g=2 * gl_key + gl_val, col_gz=ncol)
            o_p, s_p = _recurrence(y, extras, gnorm_c[j], z_gla, consts, row0=0, nseq=bsz, tb=seq,
                                   t_valid=seq, hb=gl_h, tblk=_pick_tile(seq, 512), name=f"gla_prompt_{l}", **cols)
            o_s, s_s = _recurrence(y, extras, gnorm_c[j], st_gla, consts, row0=rows_p, nseq=dec_b, s_off=j * dec_b,
                                   tb=SAMPLE_ROWS, t_valid=dec_t, hb=gl_h, sb=sample_sb, name=f"gla_sample_{l}", **cols)
            gla_p.append(s_p)
            gla_s.append(s_s)
            acts = [(o_p, o_s)]
            weights = [w_out_c[j].astype(BF16)]
        x, xb = _proj_ln(acts, weights, x, ln_g[l, 0], ln_b[l, 0], alpha=alpha, name=f"mix_out_ln_{l}")

        q = _matmul(xb, w_mq[l].astype(BF16), name=f"mem_q_{l}")
        a_p = _mem_attention(q, mk_p, mv_p, kv0=l * bsz, row0=0, nseq=bsz, tb=seq, nb=1, name=f"mem_attn_prompt_{l}")
        a_s = _mem_attention(q, ck, cv, kv0=l * dec_b, row0=rows_p, nseq=dec_b, tb=SAMPLE_ROWS,
                             nb=4 if dec_b % 4 == 0 else 1, name=f"mem_attn_sample_{l}")
        wr = jnp.concatenate([w_grp[l]] + [w_er[l, gi] for gi in range(n_groups)], axis=1)
        wr = jnp.pad(wr, ((0, 0), (0, LANES - wr.shape[1])))
        wrh = wr.astype(BF16)
        wrl = (wr - wrh.astype(F32)).astype(BF16)
        br = jnp.concatenate([b_grp[l], b_er[l].reshape(-1)])
        br = jnp.pad(br, (0, LANES - br.shape[0])).reshape(1, LANES)
        x, xb, rinfo, xt = _proj_ln([(a_p, a_s)], [w_mo[l].astype(BF16)], x, ln_g[l, 1],
                                    ln_b[l, 1], alpha=alpha, router_w=(wrh, wrl, br), router=(n_groups, n_exp),
                                    name=f"mem_out_ln_router_{l}")

        ne = n_groups * n_exp
        nsub = d // LANES
        dest, src_tok, tile_expert, n_used = _dispatch_plan(rinfo, ne)
        f = w_gate.shape[-1]
        ys = _experts(xt, src_tok, tile_expert + l * ne, n_used, w_gate.reshape(depth * ne, d, f),
                      w_up.reshape(depth * ne, d, f), w_down.reshape(depth * ne, f, d))
        x, xb = _combine_ln(ys, dest[0::TOP_K], dest[1::TOP_K], rinfo, x, ln_g[l, 2], ln_b[l, 2],
                            alpha=alpha, name=f"moe_combine_ln_{l}")

    y_prompt = x[:rows_p].reshape(bsz, seq, d)
    y_sample = x[rows_p:].reshape(dec_b, SAMPLE_ROWS, d)[:, :dec_t]
    return (y_prompt, y_sample, jnp.stack(hgrn_p), jnp.stack(conv_p), jnp.stack(gla_p),
            mk_p.reshape(depth, bsz, mlen, mem_h, mem_hd), mv_p.reshape(depth, bsz, mlen, mem_h, mem_hd),
            jnp.stack(hgrn_s), jnp.stack(conv_s), jnp.stack(gla_s))
```

```python
import functools
import math

import numpy as np
import jax
import jax.numpy as jnp
from jax import lax
from jax.experimental import pallas as pl
from jax.experimental.pallas import tpu as pltpu

F32 = jnp.float32
BF16 = jnp.bfloat16

LN_EPS = 1e-5
RMS_EPS = 1e-6
GLA_NORMALIZER = 16.0
TOP_K = 2

LANES = 128
SUBLANES = 8
CHUNK = 128
SAMPLE_ROWS = 8
EXPERT_TILE = 256
EXPERT_LOOKAHEAD = 2
VMEM_LIMIT = 48 * 1024 * 1024
MM_OUT_TILE_BYTES = 8 * 1024 * 1024

_NT = (((1,), (1,)), ((), ()))
_TN = (((0,), (0,)), ((), ()))


def _params(*sem):
    return pltpu.CompilerParams(dimension_semantics=sem, vmem_limit_bytes=VMEM_LIMIT)


def _pick_tile(n, pref):
    t = pref
    while t > SUBLANES and n % t:
        t //= 2
    assert n % t == 0, (n, pref)
    return t


def _dot(a, b):
    return jnp.dot(a, b, preferred_element_type=F32)


def _sigmoid(x):
    return 1.0 / (1.0 + jnp.exp(-x))


def _log_sigmoid(x):
    return jnp.minimum(x, 0.0) - jnp.log1p(jnp.exp(-jnp.abs(x)))


def _split2(a):
    hi = a.astype(BF16)
    lo = (a - hi.astype(F32)).astype(BF16)
    return hi, lo


def _split3(a):
    hi = a.astype(BF16)
    r = a - hi.astype(F32)
    mid = r.astype(BF16)
    lo = (r - mid.astype(F32)).astype(BF16)
    return hi, mid, lo


def _pad_rows(x, n):
    if x.shape[0] == n:
        return x
    return jnp.concatenate([x, jnp.zeros((n - x.shape[0], x.shape[1]), x.dtype)], axis=0)


def _store_token_tiles(ref, y):
    rows, d = y.shape
    nsub = d // LANES
    for s in range(nsub):
        ref[pl.ds(s, rows, stride=nsub), :] = y[:, s * LANES:(s + 1) * LANES]


def _load_token_tiles(ref, rows, d, row0=0):
    nsub = d // LANES
    return jnp.concatenate([ref[pl.ds(row0 * nsub + s, rows, stride=nsub), :] for s in range(nsub)], axis=1)


def _mm_kernel(x_ref, w_ref, o_ref):
    o_ref[...] = _dot(x_ref[...], w_ref[...]).astype(o_ref.dtype)


def _matmul(x, w, *, out_dtype=F32, name):
    n, k = x.shape
    m = w.shape[1]
    pref = 1024
    while pref > SUBLANES and pref * m * 4 > MM_OUT_TILE_BYTES:
        pref //= 2
    tm = _pick_tile(n, pref)
    tn = m
    return pl.pallas_call(
        _mm_kernel,
        grid=(n // tm, m // tn),
        in_specs=[pl.BlockSpec((tm, k), lambda i, j: (i, 0)),
                  pl.BlockSpec((k, tn), lambda i, j: (0, j))],
        out_specs=pl.BlockSpec((tm, tn), lambda i, j: (i, j)),
        out_shape=jax.ShapeDtypeStruct((n, m), out_dtype),
        compiler_params=_params("parallel", "parallel"),
        name=name,
    )(x, w)


def _mem_kv_kernel(x_ref, w_ref, o_ref, *, heads):
    y = _dot(x_ref[0], w_ref[0])
    hd = y.shape[1] // heads
    for hh in range(heads):
        o_ref[0, :, hh, :] = y[:, hh * hd:(hh + 1) * hd]


def _mem_kv(mem_bf16, w_all, heads, name):
    nb, mlen, d = mem_bf16.shape
    nj, _, dout = w_all.shape
    return pl.pallas_call(
        functools.partial(_mem_kv_kernel, heads=heads),
        grid=(nj, nb),
        in_specs=[pl.BlockSpec((1, mlen, d), lambda j, i: (i, 0, 0)),
                  pl.BlockSpec((1, d, dout), lambda j, i: (j, 0, 0))],
        out_specs=pl.BlockSpec((1, mlen, heads, dout // heads), lambda j, i: (j * nb + i, 0, 0, 0)),
        out_shape=jax.ShapeDtypeStruct((nj * nb, mlen, heads, dout // heads), F32),
        compiler_params=_params("parallel", "parallel"),
        name=name,
    )(mem_bf16, w_all)


def _layer_norm_rows(z, g, b):
    mu = jnp.mean(z, axis=-1, keepdims=True)
    zc = z - mu
    var = jnp.mean(zc * zc, axis=-1, keepdims=True)
    return zc * lax.rsqrt(var + LN_EPS) * g + b


def _route(logits, n_groups, n_exp):
    lane = lax.broadcasted_iota(jnp.int32, logits.shape, 1).astype(F32)
    big = jnp.float32(1 << 20)
    neg = jnp.float32(-jnp.inf)
    gl = jnp.where(lane < n_groups, logits, neg)
    gmax = jnp.max(gl, axis=-1, keepdims=True)
    gsel = jnp.min(jnp.where(gl == gmax, lane, big), axis=-1, keepdims=True)
    gden = jnp.sum(jnp.exp(gl - gmax), axis=-1, keepdims=True)
    g_w = 1.0 / gden
    lo = n_groups + gsel * n_exp
    el = jnp.where((lane >= lo) & (lane < lo + n_exp), logits, neg)
    m1 = jnp.max(el, axis=-1, keepdims=True)
    i1 = jnp.min(jnp.where(el == m1, lane, big), axis=-1, keepdims=True)
    el2 = jnp.where(lane == i1, neg, el)
    m2 = jnp.max(el2, axis=-1, keepdims=True)
    i2 = jnp.min(jnp.where(el2 == m2, lane, big), axis=-1, keepdims=True)
    t = jnp.exp(m2 - m1)
    w1 = g_w / (1.0 + t)
    w2 = g_w * t / (1.0 + t)
    e1 = i1 - n_groups
    e2 = i2 - n_groups
    zero = jnp.zeros(logits.shape, F32)
    return jnp.where(lane == 0, w1, jnp.where(lane == 1, w2, jnp.where(lane == 2, e1, jnp.where(lane == 3, e2, zero))))


def _proj_ln_kernel(*refs, n_in, alpha, router, prompt_tiles):
    ap_refs = refs[:n_in]
    as_refs = refs[n_in:2 * n_in]
    w_refs = refs[2 * n_in:3 * n_in]
    x_ref, g_ref, b_ref = refs[3 * n_in:3 * n_in + 3]
    pos = 3 * n_in + 3
    if router is not None:
        wrh_ref, wrl_ref, br_ref = refs[pos:pos + 3]
        pos += 3
    xf_ref, xb_ref = refs[pos:pos + 2]
    is_prompt = pl.program_id(0) < prompt_tiles
    acc = None
    for ap_ref, as_ref, w_ref in zip(ap_refs, as_refs, w_refs):
        part = _dot(jnp.where(is_prompt, ap_ref[...], as_ref[...]), w_ref[...])
        acc = part if acc is None else acc + part
    y = _layer_norm_rows(alpha * x_ref[...] + acc, g_ref[...], b_ref[...])
    xf_ref[...] = y
    xb_ref[...] = y.astype(BF16)
    if router is not None:
        hi, lo = _split2(y)
        logits = _dot(hi, wrh_ref[...]) + _dot(lo, wrh_ref[...]) + _dot(hi, wrl_ref[...]) + br_ref[...]
        refs[pos + 2][...] = _route(logits, *router)
        _store_token_tiles(refs[pos + 3], y)


def _proj_ln(acts, weights, x, g, b, *, alpha, router_w=None, router=None, name):
    n, d = x.shape
    n_p = acts[0][0].shape[0]
    tm = _pick_tile(math.gcd(n_p, n - n_p), 512)
    npt = n_p // tm
    row = lambda i: (i, 0)
    fix = lambda i: (0, 0)
    in_specs = [pl.BlockSpec((tm, a[0].shape[1]), lambda i: (jnp.minimum(i, npt - 1), 0)) for a in acts]
    in_specs += [pl.BlockSpec((tm, a[1].shape[1]), lambda i: (jnp.maximum(i - npt, 0), 0)) for a in acts]
    in_specs += [pl.BlockSpec(w.shape, fix) for w in weights]
    in_specs += [pl.BlockSpec((tm, d), row), pl.BlockSpec((1, d), fix), pl.BlockSpec((1, d), fix)]
    args = [a[0] for a in acts] + [a[1] for a in acts] + list(weights) + [x, g.reshape(1, d), b.reshape(1, d)]
    out_specs = [pl.BlockSpec((tm, d), row), pl.BlockSpec((tm, d), row)]
    out_shape = [jax.ShapeDtypeStruct((n, d), F32), jax.ShapeDtypeStruct((n, d), BF16)]
    if router_w is not None:
        in_specs += [pl.BlockSpec(r.shape, fix) for r in router_w]
        args += list(router_w)
        out_specs.append(pl.BlockSpec((tm, LANES), row))
        out_shape.append(jax.ShapeDtypeStruct((n, LANES), F32))
        out_specs.append(pl.BlockSpec((tm * (d // LANES), LANES), row))
        out_shape.append(jax.ShapeDtypeStruct((n * (d // LANES), LANES), F32))
    return pl.pallas_call(
        functools.partial(_proj_ln_kernel, n_in=len(acts), alpha=alpha, router=router, prompt_tiles=npt),
        grid=(n // tm,),
        in_specs=in_specs, out_specs=out_specs, out_shape=out_shape,
        compiler_params=_params("parallel"),
        name=name,
    )(*args)


_LEVELS = (8, 16, 32, 64)


def _rec_constants():
    t = np.arange(CHUNK)
    tri = (t[:, None] >= t[None, :]).astype(np.float32)
    masks = []
    for lvl in _LEVELS:
        bt, bs = t[:, None] // lvl, t[None, :] // lvl
        masks.append(((bt - bs == 1) & (bt % 2 == 1)).astype(np.float32))
    return jnp.asarray(tri, BF16), jnp.asarray(np.stack(masks), F32)


def _rec_chunk(q, k, a, v, st, tri_ref, mask_ref, nv):
    r = SUBLANES * nv
    m = max(r, 2 * SUBLANES)
    dk = q.shape[1]

    if nv == 1:
        row_k = lax.broadcasted_iota(jnp.int32, (SUBLANES, dk), 0)
        row_v = lax.broadcasted_iota(jnp.int32, v.shape, 0)
        b = a
        for sh in (1, 2, 4):
            b = b + jnp.where(row_k >= sh, pltpu.roll(b, sh, axis=0), 0.0)
        b_last = b[r - 1:r]
        o = jnp.zeros(v.shape, F32)
        for s_ in range(SUBLANES):
            e = jnp.exp(jnp.minimum(b - b[s_:s_ + 1], 0.0))
            col = jnp.sum(q * k[s_:s_ + 1] * e, axis=1, keepdims=True)
            o = o + jnp.where(row_v >= s_, col, 0.0) * v[s_:s_ + 1]
        qe = _pad_rows(q * jnp.exp(b), m).astype(BF16)
        o = o + lax.dot_general(qe, st.astype(BF16), _NT, preferred_element_type=F32)[0:r]
        kd = _pad_rows(k * jnp.exp(b_last - b), CHUNK).astype(BF16)
        v_bf = _pad_rows(v, CHUNK).astype(BF16)
        st_new = st * jnp.exp(b_last) + lax.dot_general(v_bf, kd, _TN, preferred_element_type=F32)
        return o, st_new

    hi, mid, lo = _split3(_pad_rows(a, CHUNK))
    tri = tri_ref[0:m, :]
    b = (_dot(tri, hi) + _dot(tri, mid) + _dot(tri, lo))[0:r]
    b_last = b[r - 1:r]

    scores = None
    for li, lvl in enumerate(_LEVELS):
        if lvl >= r:
            continue
        qp, kp = [], []
        for blk in range(r // lvl):
            rows = slice(blk * lvl, (blk + 1) * lvl)
            if blk % 2 == 1:
                rho = b[blk * lvl - 1:blk * lvl]
                qp.append(q[rows] * jnp.exp(b[rows] - rho))
                kp.append(jnp.zeros((lvl, dk), F32))
            else:
                rho = b[(blk + 1) * lvl - 1:(blk + 1) * lvl]
                kp.append(k[rows] * jnp.exp(rho - b[rows]))
                qp.append(jnp.zeros((lvl, dk), F32))
        qh = jnp.concatenate(qp, axis=0).astype(BF16)
        kh = jnp.concatenate(kp, axis=0).astype(BF16)
        s = lax.dot_general(qh, kh, _NT, preferred_element_type=F32) * mask_ref[li]
        scores = s if scores is None else scores + s

    lane = lax.broadcasted_iota(jnp.int32, (SUBLANES, CHUNK), 1)
    sub = lax.broadcasted_iota(jnp.int32, (SUBLANES, CHUNK), 0)
    dparts = []
    for j in range(nv):
        rows = slice(SUBLANES * j, SUBLANES * (j + 1))
        qj, kj, bj = q[rows], k[rows], b[rows]
        acc = jnp.zeros((SUBLANES, CHUNK), F32)
        for s_ in range(SUBLANES):
            e = jnp.exp(jnp.minimum(bj - bj[s_:s_ + 1], 0.0))
            col = jnp.sum(qj * kj[s_:s_ + 1] * e, axis=1, keepdims=True)
            acc = jnp.where(lane == SUBLANES * j + s_, col, acc)
        dparts.append(jnp.where(lane <= sub + SUBLANES * j, acc, 0.0))
    diag = _pad_rows(jnp.concatenate(dparts, axis=0) if nv > 1 else dparts[0], m)
    scores = diag if scores is None else scores + diag

    v_bf = _pad_rows(v, CHUNK).astype(BF16)
    o = _dot(scores.astype(BF16), v_bf)
    qe = _pad_rows(q * jnp.exp(b), m).astype(BF16)
    o = o + lax.dot_general(qe, st.astype(BF16), _NT, preferred_element_type=F32)
    kd = _pad_rows(k * jnp.exp(b_last - b), CHUNK).astype(BF16)
    st_new = st * jnp.exp(b_last) + lax.dot_general(v_bf, kd, _TN, preferred_element_type=F32)
    return o[0:r], st_new


def _rec_kernel(*refs, mode, hb, sb, dk, dv, nv, nchunks, t_valid, qscale):
    q_ref, k_ref, v_ref, g_ref = refs[:4]
    pos = 4
    if mode == "gla":
        gz_ref, wgh_ref, wgl_ref, bg_ref = refs[pos:pos + 4]
        pos += 4
    else:
        loglb_ref, log1m_ref, onem_ref = refs[pos:pos + 3]
        pos += 3
    nw_ref, s0_ref, tri_ref, mask_ref = refs[pos:pos + 4]
    o_ref, so_ref, st_scr = refs[pos + 4:pos + 7]
    r = SUBLANES * nv
    tstep = pl.program_id(2)

    @pl.when(tstep == 0)
    def _():
        for sq in range(sb):
            for hh in range(hb):
                st_scr[sq * hb + hh] = s0_ref[sq, hh].T

    def chunk_step(c, carry, sq=0):
        rsl = pl.ds(c * r if isinstance(c, int) else pl.multiple_of(c * r, r), r)
        if t_valid < r:
            valid_k = lax.broadcasted_iota(jnp.int32, (r, dk), 0) < t_valid
            valid_v = lax.broadcasted_iota(jnp.int32, (r, dv), 0) < t_valid
        if mode == "gla":
            gz = _pad_rows(gz_ref[rsl, :], max(r, 2 * SUBLANES))
            gzh, gzl = _split2(gz)
        for hh in range(hb):
            ksl = slice(hh * dk, (hh + 1) * dk)
            vsl = slice(hh * dv, (hh + 1) * dv)
            qr = q_ref[rsl, ksl]
            kr = k_ref[rsl, ksl]
            v = v_ref[rsl, vsl]
            if mode == "gla":
                gk = (_dot(gzh, wgh_ref[:, ksl]) + _dot(gzl, wgh_ref[:, ksl]) + _dot(gzh, wgl_ref[:, ksl]))[0:r]
                a = _log_sigmoid(gk + bg_ref[:, ksl]) / GLA_NORMALIZER
                q = qr * qscale
                k = kr
            else:
                ls = _log_sigmoid(kr)
                x2 = log1m_ref[:, ksl] + ls
                x1 = loglb_ref[:, ksl]
                a = jnp.maximum(x1, x2) + jnp.log1p(jnp.exp(-jnp.abs(x1 - x2)))
                k = onem_ref[:, ksl] * _sigmoid(-kr)
                q = qr * _sigmoid(qr)
            if t_valid < r:
                a = jnp.where(valid_k, a, 0.0)
                k = jnp.where(valid_k, k, 0.0)
                v = jnp.where(valid_v, v, 0.0)
            o, st_new = _rec_chunk(q, k, a, v, st_scr[sq * hb + hh], tri_ref, mask_ref, nv)
            st_scr[sq * hb + hh] = st_new
            g = g_ref[rsl, vsl]
            y = o * lax.rsqrt(jnp.mean(o * o, axis=-1, keepdims=True) + RMS_EPS) * nw_ref[...]
            o_ref[rsl, vsl] = (y * (g * _sigmoid(g))).astype(o_ref.dtype)
        return carry

    if nchunks == 1:
        for sq in range(sb):
            chunk_step(sq, 0, sq)
    else:
        lax.fori_loop(0, nchunks, chunk_step, 0, unroll=2)

    @pl.when(tstep == pl.num_programs(2) - 1)
    def _():
        for sq in range(sb):
            for hh in range(hb):
                so_ref[sq, hh] = st_scr[sq * hb + hh].T


def _recurrence(y, extras, norm_w, s0, consts, *, mode, col_q, col_k, col_v, col_g, col_gz=None,
                row0, nseq, tb, t_valid, hb, name, s_off=0, tblk=None, sb=1):
    _, h, dk, dv = s0.shape
    tblk = tb if tblk is None else tblk
    nt = tb // tblk
    nv = min(tblk, CHUNK) // SUBLANES
    nchunks = tblk // (SUBLANES * nv)
    assert h % hb == 0 and tb % tblk == 0 and nseq % sb == 0 and s_off % sb == 0
    assert sb == 1 or (nt == 1 and nchunks == 1)
    rblk = sb * tblk
    assert row0 % rblk == 0
    rb0 = row0 // rblk
    sblk0 = s_off // sb
    kw, vw = hb * dk, hb * dv
    tri, masks = consts

    def cspec(width, col):
        assert col % width == 0
        cb = col // width
        return pl.BlockSpec((rblk, width), lambda i, j, t: (rb0 + i * nt + t, cb + j))

    fixh = lambda width: pl.BlockSpec((1, width), lambda i, j, t: (0, j))
    in_specs = [cspec(kw, col_q), cspec(kw, col_k), cspec(vw, col_v), cspec(vw, col_g)]
    args = [y, y, y, y]
    if mode == "gla":
        wgh, wgl, bg = extras
        in_specs += [pl.BlockSpec((rblk, LANES), lambda i, j, t: (rb0 + i * nt + t, col_gz // LANES)),
                     pl.BlockSpec((LANES, kw), lambda i, j, t: (0, j)),
                     pl.BlockSpec((LANES, kw), lambda i, j, t: (0, j)), fixh(kw)]
        args += [y, wgh, wgl, bg]
    else:
        in_specs += [fixh(kw), fixh(kw), fixh(kw)]
        args += list(extras)
    in_specs += [pl.BlockSpec((1, dv), lambda i, j, t: (0, 0)),
                 pl.BlockSpec((sb, hb, dk, dv), lambda i, j, t: (sblk0 + i, j, 0, 0)),
                 pl.BlockSpec(tri.shape, lambda i, j, t: (0, 0)),
                 pl.BlockSpec(masks.shape, lambda i, j, t: (0, 0, 0))]
    args += [norm_w.reshape(1, dv), s0, tri, masks]
    return pl.pallas_call(
        functools.partial(_rec_kernel, mode=mode, hb=hb, sb=sb, dk=dk, dv=dv, nv=nv, nchunks=nchunks,
                          t_valid=t_valid, qscale=dk ** -0.5),
        grid=(nseq // sb, h // hb, nt),
        in_specs=in_specs,
        out_specs=[pl.BlockSpec((rblk, vw), lambda i, j, t: (i * nt + t, j)),
                   pl.BlockSpec((sb, hb, dk, dv), lambda i, j, t: (i, j, 0, 0))],
        out_shape=[jax.ShapeDtypeStruct((nseq * tb, h * dv), BF16),
                   jax.ShapeDtypeStruct((nseq, h, dk, dv), F32)],
        scratch_shapes=[pltpu.VMEM((sb * hb, dv, dk), F32)],
        compiler_params=_params("parallel", "parallel", "arbitrary"),
        name=name,
    )(*args)


def _conv_kernel(bg_ref, cg_ref, h_ref, w_ref, buf_ref, y_ref, tail_ref, carry):
    t = pl.program_id(1)
    u = cg_ref[...] * h_ref[...]
    tt = u.shape[0]

    @pl.when(t == 0)
    def _():
        carry[...] = jnp.zeros(carry.shape, F32)
        carry[SUBLANES - 2:SUBLANES, :] = buf_ref[0]

    p6 = carry[SUBLANES - 2:SUBLANES - 1, :]
    p7 = carry[SUBLANES - 1:SUBLANES, :]
    row = lax.broadcasted_iota(jnp.int32, u.shape, 0)
    u1 = jnp.where(row == 0, p7, pltpu.roll(u, 1, axis=0))
    u2 = jnp.where(row == 0, p6, jnp.where(row == 1, p7, pltpu.roll(u, 2, axis=0)))
    y = bg_ref[...] * (u2 * w_ref[0:1, :] + u1 * w_ref[1:2, :] + u * w_ref[2:3, :])
    y_ref[...] = y.astype(y_ref.dtype)
    tail = u[tt - SUBLANES:tt, :]
    carry[...] = tail
    tail_ref[0] = tail


def _short_conv(y, w3, buf, *, col_b, col_c, col_h, width, row0, nseq, tb, name):
    tt = _pick_tile(tb, 512)
    nt = tb // tt
    rb0 = row0 // tt

    def cspec(col):
        cb = col // width
        return pl.BlockSpec((tt, width), lambda i, t: (rb0 + i * nt + t, cb))

    return pl.pallas_call(
        _conv_kernel,
        grid=(nseq, nt),
        in_specs=[cspec(col_b), cspec(col_c), cspec(col_h),
                  pl.BlockSpec((3, width), lambda i, t: (0, 0)),
                  pl.BlockSpec((1, 2, width), lambda i, t: (i, 0, 0))],
        out_specs=[pl.BlockSpec((tt, width), lambda i, t: (i * nt + t, 0)),
                   pl.BlockSpec((1, SUBLANES, width), lambda i, t: (i, 0, 0))],
        out_shape=[jax.ShapeDtypeStruct((nseq * tb, width), BF16),
                   jax.ShapeDtypeStruct((nseq, SUBLANES, width), F32)],
        scratch_shapes=[pltpu.VMEM((SUBLANES, width), F32)],
        compiler_params=_params("parallel", "arbitrary"),
        name=name,
    )(y, y, y, w3, buf)


def _attn_kernel(q_ref, k_ref, v_ref, o_ref, kscr, vscr, *, nb, rows, heads, hd):
    scale = hd ** -0.5
    m = max(rows, 2 * SUBLANES)
    mlen = k_ref.shape[1]
    for bi in range(nb):
        k2 = k_ref[bi].reshape(mlen * heads, hd)
        v2 = v_ref[bi].reshape(mlen * heads, hd)
        for c in range(hd // LANES):
            kscr[bi, c] = k2[:, c * LANES:(c + 1) * LANES]
            vscr[bi, c] = v2[:, c * LANES:(c + 1) * LANES]
        qb = _pad_rows(q_ref[bi * rows:(bi + 1) * rows, :], m)
        outs = []
        for hh in range(heads):
            sl = slice(hh * hd, (hh + 1) * hd)
            head_rows = pl.ds(hh, mlen, stride=heads)
            kh = jnp.concatenate([kscr[bi, c, head_rows, :] for c in range(hd // LANES)], axis=1).astype(BF16)
            vh = jnp.concatenate([vscr[bi, c, head_rows, :] for c in range(hd // LANES)], axis=1).astype(BF16)
            s = lax.dot_general(qb[:, sl].astype(BF16), kh, _NT, preferred_element_type=F32) * scale
            p = jnp.exp(s - jnp.max(s, axis=-1, keepdims=True))
            p = p / jnp.sum(p, axis=-1, keepdims=True)
            outs.append(_dot(p.astype(BF16), vh))
        o = jnp.concatenate(outs, axis=-1)
        o_ref[bi * rows:(bi + 1) * rows, :] = o[0:rows].astype(o_ref.dtype)


def _attn_short_kernel(q_ref, k_ref, v_ref, o_ref, *, nb, rows, heads, hd):
    scale = hd ** -0.5
    mlen = k_ref.shape[1]
    nrow, ncol = heads * rows, mlen * heads
    col_head = lax.broadcasted_iota(jnp.int32, (nrow, ncol), 1) & (heads - 1)
    row_head = lax.shift_right_logical(lax.broadcasted_iota(jnp.int32, (nrow, ncol), 0),
                                       jnp.int32(rows.bit_length() - 1))
    own = col_head == row_head
    for bi in range(nb):
        k2 = k_ref[bi].reshape(ncol, hd).astype(BF16)
        v2 = v_ref[bi].reshape(ncol, hd).astype(BF16)
        qf = q_ref[bi * rows:(bi + 1) * rows, :]
        qs = jnp.concatenate([qf[:, hh * hd:(hh + 1) * hd] for hh in range(heads)], axis=0).astype(BF16)
        s = lax.dot_general(qs, k2, _NT, preferred_element_type=F32) * scale
        s = jnp.where(own, s, -jnp.inf)
        p = jnp.exp(s - jnp.max(s, axis=-1, keepdims=True))
        p = p / jnp.sum(p, axis=-1, keepdims=True)
        o2 = _dot(p.astype(BF16), v2)
        o = jnp.concatenate([o2[hh * rows:(hh + 1) * rows] for hh in range(heads)], axis=1)
        o_ref[bi * rows:(bi + 1) * rows, :] = o.astype(o_ref.dtype)


def _mem_attention(q, mem_k, mem_v, *, kv0, row0, nseq, tb, nb, name):
    d = q.shape[1]
    _, mlen, heads, hd = mem_k.shape
    assert kv0 % nb == 0
    kvb = kv0 // nb
    if nb > 1:
        rows, nt = tb, 1
    else:
        rows = _pick_tile(tb, 512)
        nt = tb // rows
    assert nseq % nb == 0 and row0 % (nb * rows) == 0
    rb0 = row0 // (nb * rows)
    short = nb > 1 and rows % SUBLANES == 0 and heads & (heads - 1) == 0 and rows & (rows - 1) == 0
    scratch = [] if short else [pltpu.VMEM((nb, hd // LANES, mlen * heads, LANES), F32)] * 2
    return pl.pallas_call(
        functools.partial(_attn_short_kernel if short else _attn_kernel, nb=nb, rows=rows, heads=heads, hd=hd),
        grid=(nseq // nb, nt),
        in_specs=[pl.BlockSpec((nb * rows, d), lambda i, t: (rb0 + i * nt + t, 0)),
                  pl.BlockSpec((nb, mlen, heads, hd), lambda i, t: (kvb + i, 0, 0, 0)),
                  pl.BlockSpec((nb, mlen, heads, hd), lambda i, t: (kvb + i, 0, 0, 0))],
        out_specs=pl.BlockSpec((nb * rows, d), lambda i, t: (i * nt + t, 0)),
        out_shape=jax.ShapeDtypeStruct((nseq * tb, d), BF16),
        scratch_shapes=scratch,
        compiler_params=_params("parallel", "parallel"),
        name=name,
    )(q, mem_k, mem_v)


def _tile_copy(src_hbm, tok, dst, r, sem, nsub):
    so = tok * nsub if isinstance(tok, int) else pl.multiple_of(tok * nsub, nsub)
    do = pl.multiple_of(r * nsub, nsub)
    return pltpu.make_async_copy(src_hbm.at[pl.ds(so, nsub)], dst.at[pl.ds(do, nsub)], sem)


def _gather_start(src_hbm, idx_ref, n, dst, sem, nsub):
    def body(g, c):
        for u in range(SUBLANES):
            r = g * SUBLANES + u
            _tile_copy(src_hbm, idx_ref[r], dst, r, sem, nsub).start(priority=u % 2)
        return c

    lax.fori_loop(0, n // SUBLANES, body, 0)


def _gather_wait(src_hbm, n, dst, sem, nsub):
    def body(r, c):
        _tile_copy(src_hbm, 0, dst, r, sem, nsub).wait()
        return c

    lax.fori_loop(0, n, body, 0, unroll=8)


def _expert_kernel(te_ref, nu_ref, *refs, d):
    depth = EXPERT_LOOKAHEAD
    idx_refs = refs[:depth + 1]
    x_hbm, wg_ref, wu_ref, wd_ref, o_ref, xbuf, wgb, wub, wdb, sem = refs[depth + 1:]
    i = pl.program_id(0)
    nu = nu_ref[0]
    nsub = d // LANES

    for k in range(depth):
        @pl.when((i == 0) & (k < nu))
        def _(k=k):
            _gather_start(x_hbm, idx_refs[k], EXPERT_TILE, xbuf.at[k], sem.at[k], nsub)

    @pl.when(i + depth < nu)
    def _():
        nxt = (i + depth) % (depth + 1)
        _gather_start(x_hbm, idx_refs[depth], EXPERT_TILE, xbuf.at[nxt], sem.at[nxt], nsub)

    @pl.when(i < nu)
    def _():
        @pl.when((i == 0) | (te_ref[i] != te_ref[jnp.maximum(i - 1, 0)]))
        def _():
            wgb[...] = wg_ref[0].astype(BF16)
            wub[...] = wu_ref[0].astype(BF16)
            wdb[...] = wd_ref[0].astype(BF16)

        slot = i % (depth + 1)
        _gather_wait(x_hbm, EXPERT_TILE, xbuf.at[slot], sem.at[slot], nsub)
        x = _load_token_tiles(xbuf.at[slot], EXPERT_TILE, d).astype(BF16)
        gate = _dot(x, wgb[...])
        up = _dot(x, wub[...])
        hid = (gate * _sigmoid(gate) * up).astype(BF16)
        _store_token_tiles(o_ref, _dot(hid, wdb[...]))

    @pl.when(i >= nu)
    def _():
        o_ref[...] = jnp.zeros(o_ref.shape, F32)


def _experts(xt, src_tok, tile_expert, n_used, wg, wu, wd):
    d, f = wg.shape[1], wg.shape[2]
    nsub = d // LANES
    p = src_tok.shape[0]
    nt = p // EXPERT_TILE
    live = lambda i, te, nu: jnp.minimum(i, nu[0] - 1)
    wspec = lambda shp: pl.BlockSpec(shp, lambda i, te, nu: (te[live(i, te, nu)], 0, 0))
    depth = EXPERT_LOOKAHEAD
    idx_spec = lambda k: pl.BlockSpec((EXPERT_TILE,), lambda i, te, nu: (jnp.minimum(i + k, nt - 1),),
                                      memory_space=pltpu.SMEM)
    grid_spec = pltpu.PrefetchScalarGridSpec(
        num_scalar_prefetch=2,
        grid=(nt,),
        in_specs=[idx_spec(k) for k in range(depth + 1)]
        + [pl.BlockSpec(memory_space=pl.ANY), wspec((1, d, f)), wspec((1, d, f)), wspec((1, f, d))],
        out_specs=pl.BlockSpec((EXPERT_TILE * nsub, LANES), lambda i, te, nu: (i, 0)),
        scratch_shapes=[pltpu.VMEM((depth + 1, EXPERT_TILE * nsub, LANES), F32),
                        pltpu.VMEM((d, f), BF16), pltpu.VMEM((d, f), BF16), pltpu.VMEM((f, d), BF16),
                        pltpu.SemaphoreType.DMA((depth + 1,))],
    )
    return pl.pallas_call(
        functools.partial(_expert_kernel, d=d),
        grid_spec=grid_spec,
        out_shape=jax.ShapeDtypeStruct((p * nsub, LANES), F32),
        compiler_params=_params("arbitrary"),
        name="moe_experts",
    )(tile_expert, n_used, *([src_tok] * (depth + 1)), xt, wg, wu, wd)


def _combine_ln_kernel(ia0_ref, ia1_ref, ib0_ref, ib1_ref, y_hbm, r_ref, x_ref, g_ref, b_ref, xf_ref, xb_ref,
                       ybuf, sem, *, alpha, nsteps):
    i = pl.program_id(0)
    rows, d = x_ref.shape
    nsub = d // LANES

    def start(idx_a, idx_b, slot):
        _gather_start(y_hbm, idx_a, rows, ybuf.at[slot, 0], sem.at[slot, 0], nsub)
        _gather_start(y_hbm, idx_b, rows, ybuf.at[slot, 1], sem.at[slot, 1], nsub)

    @pl.when(i == 0)
    def _():
        start(ia0_ref, ib0_ref, 0)

    @pl.when(i + 1 < nsteps)
    def _():
        start(ia1_ref, ib1_ref, (i + 1) % 2)

    slot = i % 2
    _gather_wait(y_hbm, rows, ybuf.at[slot, 0], sem.at[slot, 0], nsub)
    _gather_wait(y_hbm, rows, ybuf.at[slot, 1], sem.at[slot, 1], nsub)
    w0 = r_ref[:, 0:1]
    w1 = r_ref[:, 1:2]
    moe = w0 * _load_token_tiles(ybuf.at[slot, 0], rows, d) + w1 * _load_token_tiles(ybuf.at[slot, 1], rows, d)
    y = _layer_norm_rows(alpha * x_ref[...] + moe, g_ref[...], b_ref[...])
    xf_ref[...] = y
    xb_ref[...] = y.astype(BF16)


def _combine_ln(ys, dest_a, dest_b, rinfo, x, g, b, *, alpha, name):
    n, d = x.shape
    nsub = d // LANES
    tm = _pick_tile(n, 512)
    nsteps = n // tm
    row = lambda i: (i, 0)
    fix = lambda i: (0, 0)
    cur = pl.BlockSpec((tm,), lambda i: (i,), memory_space=pltpu.SMEM)
    nxt = pl.BlockSpec((tm,), lambda i: (jnp.minimum(i + 1, nsteps - 1),), memory_space=pltpu.SMEM)
    return pl.pallas_call(
        functools.partial(_combine_ln_kernel, alpha=alpha, nsteps=nsteps),
        grid=(nsteps,),
        in_specs=[cur, nxt, cur, nxt, pl.BlockSpec(memory_space=pl.ANY),
                  pl.BlockSpec((tm, LANES), row), pl.BlockSpec((tm, d), row),
                  pl.BlockSpec((1, d), fix), pl.BlockSpec((1, d), fix)],
        out_specs=[pl.BlockSpec((tm, d), row), pl.BlockSpec((tm, d), row)],
        out_shape=[jax.ShapeDtypeStruct((n, d), F32), jax.ShapeDtypeStruct((n, d), BF16)],
        scratch_shapes=[pltpu.VMEM((2, 2, tm * nsub, LANES), F32), pltpu.SemaphoreType.DMA((2, 2))],
        compiler_params=_params("arbitrary"),
        name=name,
    )(dest_a, dest_a, dest_b, dest_b, ys, rinfo, x, g.reshape(1, d), b.reshape(1, d))


def _dispatch_plan(rinfo, n_exp_total):
    n = rinfo.shape[0]
    a = n * TOP_K
    eid = rinfo[:, 2:2 + TOP_K].astype(jnp.int32).reshape(a)
    onehot = (eid[:, None] == jnp.arange(n_exp_total, dtype=jnp.int32)[None, :]).astype(jnp.int32)
    csum = jnp.cumsum(onehot, axis=0)
    rank = jnp.sum(csum * onehot, axis=1) - 1
    counts = csum[-1]
    padded = ((counts + EXPERT_TILE - 1) // EXPERT_TILE) * EXPERT_TILE
    ends = jnp.cumsum(padded)
    starts = ends - padded
    dest = starts[eid] + rank
    nt = -(-a // EXPERT_TILE) + n_exp_total
    p = nt * EXPERT_TILE
    src_tok = jnp.zeros((p,), jnp.int32).at[dest].set(jnp.arange(a, dtype=jnp.int32) // TOP_K)
    tile_start = jnp.arange(nt, dtype=jnp.int32) * EXPERT_TILE
    tile_expert = jnp.minimum(jnp.sum(tile_start[:, None] >= ends[None, :], axis=1), n_exp_total - 1).astype(jnp.int32)
    n_used = (ends[-1] // EXPERT_TILE).astype(jnp.int32).reshape(1)
    return dest.astype(jnp.int32), src_tok, tile_expert, n_used


def kernel(x_prompt, x_sample, state_hgrn, state_conv, state_gla, cache_mem_k, cache_mem_v, mem_prompt, w_in_a, lb_a, conv_w, gnorm_a, w_out_a, w_in_c, w_gk2, b_gk2, gnorm_c, w_out_c, w_mq, w_mk, w_mv, w_mo, ln_g, ln_b, w_grp, b_grp, w_er, b_er, w_gate, w_up, w_down):
    bsz, seq, d = x_prompt.shape
    dec_b, dec_t, _ = x_sample.shape
    depth = w_mq.shape[0]
    _, _, ha_h, ha_dk, ha_dv = state_hgrn.shape
    _, _, gl_h, gl_dk, gl_dv = state_gla.shape
    _, _, mlen, mem_h, mem_hd = cache_mem_k.shape
    n_groups, n_exp = w_er.shape[1], w_er.shape[3]
    sc_w = conv_w.shape[1]
    sc_k = conv_w.shape[2]
    ha_f, ha_w = ha_h * ha_dk, ha_h * ha_dv
    gl_key, gl_val = gl_h * gl_dk, gl_h * gl_dv
    rank = w_gk2.shape[1]
    alpha = (2.0 * depth) ** 0.25
    assert dec_t <= SAMPLE_ROWS and sc_k == 3 and seq % CHUNK == 0 and rank <= LANES
    assert n_groups + n_groups * n_exp <= LANES

    rows_p = bsz * seq
    rows_s = dec_b * SAMPLE_ROWS
    xs_pad = jnp.pad(x_sample, ((0, 0), (0, SAMPLE_ROWS - dec_t), (0, 0)))
    x = jnp.concatenate([x_prompt.reshape(rows_p, d), xs_pad.reshape(rows_s, d)], axis=0)
    xb = x.astype(BF16)
    consts = _rec_constants()
    sample_sb = 4 if dec_b % 4 == 0 else 1

    mem_b = mem_prompt.astype(BF16)
    mk_p = _mem_kv(mem_b, w_mk.astype(BF16), mem_h, "mem_k_proj")
    mv_p = _mem_kv(mem_b, w_mv.astype(BF16), mem_h, "mem_v_proj")
    ck = cache_mem_k.reshape(depth * dec_b, mlen, mem_h, mem_hd)
    cv = cache_mem_v.reshape(depth * dec_b, mlen, mem_h, mem_hd)
    st_hgrn = state_hgrn.reshape(-1, ha_h, ha_dk, ha_dv)
    st_gla = state_gla.reshape(-1, gl_h, gl_dk, gl_dv)

    lb = jnp.cumsum(jax.nn.softmax(lb_a.astype(F32), axis=0), axis=0)
    lb = lb - lb[0]
    log_lb, log1m_lb, onem_lb = jnp.log(lb), jnp.log1p(-lb), 1.0 - lb

    z_hgrn = jnp.zeros((bsz, ha_h, ha_dk, ha_dv), F32)
    z_conv = jnp.zeros((bsz, sc_k - 1, sc_w), F32)
    z_gla = jnp.zeros((bsz, gl_h, gl_dk, gl_dv), F32)

    hgrn_p, conv_p, gla_p, hgrn_s, conv_s, gla_s = [], [], [], [], [], []
    for l in range(depth):
        j = l // 2
        if l % 2 == 0:
            y = _matmul(xb, w_in_a[j].astype(BF16), name=f"in_proj_{l}")
            extras = (log_lb[j:j + 1], log1m_lb[j:j + 1], onem_lb[j:j + 1])
            cols = dict(mode="hgrn", col_q=0, col_k=ha_f, col_v=2 * ha_f, col_g=2 * ha_f + ha_w)
            o_p, s_p = _recurrence(y, extras, gnorm_a[j], z_hgrn, consts, row0=0, nseq=bsz, tb=seq,
                                   t_valid=seq, hb=ha_h, tblk=_pick_tile(seq, 512), name=f"hgrn_prompt_{l}", **cols)
            o_s, s_s = _recurrence(y, extras, gnorm_a[j], st_hgrn, consts, row0=rows_p, nseq=dec_b, s_off=j * dec_b,
                                   tb=SAMPLE_ROWS, t_valid=dec_t, hb=ha_h, sb=sample_sb, name=f"hgrn_sample_{l}", **cols)
            c0 = 2 * ha_f + 2 * ha_w
            w3 = conv_w[j].T
            ccols = dict(col_b=c0, col_c=c0 + sc_w, col_h=c0 + 2 * sc_w, width=sc_w)
            c_p, t_p = _short_conv(y, w3, z_conv, row0=0, nseq=bsz, tb=seq, name=f"conv_prompt_{l}", **ccols)
            c_s, t_s = _short_conv(y, w3, state_conv[j], row0=rows_p, nseq=dec_b, tb=SAMPLE_ROWS,
                                   name=f"conv_sample_{l}", **ccols)
            hgrn_p.append(s_p)
            hgrn_s.append(s_s)
            conv_p.append(t_p[:, SUBLANES - (sc_k - 1):])
            conv_s.append(t_s[:, dec_t - (sc_k - 1):dec_t])
            acts = [(o_p, o_s), (c_p, c_s)]
            wo = w_out_a[j].astype(BF16)
            weights = [wo[:ha_w], wo[ha_w:]]
        else:
            ncol = 2 * gl_key + 2 * gl_val
            w_in = jnp.pad(w_in_c[j], ((0, 0), (0, LANES - rank))).astype(BF16)
            y = _matmul(xb, w_in, name=f"in_proj_{l}")
            wg2 = jnp.pad(w_gk2[j], ((0, LANES - rank), (0, 0)))
            wgh = wg2.astype(BF16)
            wgl = (wg2 - wgh.astype(F32)).astype(BF16)
            extras = (wgh, wgl, b_gk2[j].reshape(1, gl_key))
            cols = dict(mode="gla", col_q=0, col_k=gl_key, col_v=2 * gl_key, col_g=2 * gl_key + gl_val, col_gz=ncol)
            o_p, s_p = _recurrence(y, extras, gnorm_c[j], z_gla, consts, row0=0, nseq=bsz, tb=seq,
                                   t_valid=seq, hb=gl_h, tblk=_pick_tile(seq, 512), name=f"gla_prompt_{l}", **cols)
            o_s, s_s = _recurrence(y, extras, gnorm_c[j], st_gla, consts, row0=rows_p, nseq=dec_b, s_off=j * dec_b,
                                   tb=SAMPLE_ROWS, t_valid=dec_t, hb=gl_h, sb=sample_sb, name=f"gla_sample_{l}", **cols)
            gla_p.append(s_p)
            gla_s.append(s_s)
            acts = [(o_p, o_s)]
            weights = [w_out_c[j].astype(BF16)]
        x, xb = _proj_ln(acts, weights, x, ln_g[l, 0], ln_b[l, 0], alpha=alpha, name=f"mix_out_ln_{l}")

        q = _matmul(xb, w_mq[l].astype(BF16), name=f"mem_q_{l}")
        a_p = _mem_attention(q, mk_p, mv_p, kv0=l * bsz, row0=0, nseq=bsz, tb=seq, nb=1, name=f"mem_attn_prompt_{l}")
        a_s = _mem_attention(q, ck, cv, kv0=l * dec_b, row0=rows_p, nseq=dec_b, tb=SAMPLE_ROWS,
                             nb=4 if dec_b % 4 == 0 else 1, name=f"mem_attn_sample_{l}")
        wr = jnp.concatenate([w_grp[l]] + [w_er[l, gi] for gi in range(n_groups)], axis=1)
        wr = jnp.pad(wr, ((0, 0), (0, LANES - wr.shape[1])))
        wrh = wr.astype(BF16)
        wrl = (wr - wrh.astype(F32)).astype(BF16)
        br = jnp.concatenate([b_grp[l], b_er[l].reshape(-1)])
        br = jnp.pad(br, (0, LANES - br.shape[0])).reshape(1, LANES)
        x, xb, rinfo, xt = _proj_ln([(a_p, a_s)], [w_mo[l].astype(BF16)], x, ln_g[l, 1],
                                    ln_b[l, 1], alpha=alpha, router_w=(wrh, wrl, br), router=(n_groups, n_exp),
                                    name=f"mem_out_ln_router_{l}")

        ne = n_groups * n_exp
        nsub = d // LANES
        dest, src_tok, tile_expert, n_used = _dispatch_plan(rinfo, ne)
        f = w_gate.shape[-1]
        ys = _experts(xt, src_tok, tile_expert + l * ne, n_used, w_gate.reshape(depth * ne, d, f),
                      w_up.reshape(depth * ne, d, f), w_down.reshape(depth * ne, f, d))
        x, xb = _combine_ln(ys, dest[0::TOP_K], dest[1::TOP_K], rinfo, x, ln_g[l, 2], ln_b[l, 2],
                            alpha=alpha, name=f"moe_combine_ln_{l}")

    y_prompt = x[:rows_p].reshape(bsz, seq, d)
    y_sample = x[rows_p:].reshape(dec_b, SAMPLE_ROWS, d)[:, :dec_t]
    return (y_prompt, y_sample, jnp.stack(hgrn_p), jnp.stack(conv_p), jnp.stack(gla_p),
            mk_p.reshape(depth, bsz, mlen, mem_h, mem_hd), mv_p.reshape(depth, bsz, mlen, mem_h, mem_hd),
            jnp.stack(hgrn_s), jnp.stack(conv_s), jnp.stack(gla_s))
```
